```python
import jax, jax.numpy as jnp
from jax import lax
import numpy as np

D_MODEL = 1024
BATCH = 4
SEQ = 8192
DEPTH = 1

CHUNK = 64
EPS = 1e-6

HG_WIDTH = D_MODEL // 2
HG_DK = 128
HG_DV = 128
HG_HEADS = HG_WIDTH // HG_DK

RW_WIDTH = D_MODEL - HG_WIDTH
RW_HEAD = 64
RW_HEADS = RW_WIDTH // RW_HEAD
RW_DECAY_LORA = 64
RW_A_LORA = 64
RW_GATE_LORA = 128
RW_GN_EPS = 64e-5

HG_COLS = 4 * HG_WIDTH
RW_COLS = 3 * RW_WIDTH + RW_DECAY_LORA + RW_A_LORA + RW_GATE_LORA
IN_COLS = HG_COLS + RW_COLS

D_FF = ((8 * D_MODEL // 3 + 255) // 256) * 256

kernel_name = 'hymba_hgrn2_rwkv7_hybrid_block'


def rmsnorm(x, w):
    xf = x.astype(jnp.float32)
    y = xf * lax.rsqrt(jnp.mean(xf * xf, axis=-1, keepdims=True) + EPS)
    return (y * w.astype(jnp.float32)).astype(x.dtype)


def token_shift(p):
    return jnp.pad(p[:, :-1], ((0, 0), (1, 0), (0, 0)))


def hgrn2_mixer(q, f_logit, i, g, lb, norm_w):
    B, T, _ = q.shape
    N = T // CHUNK
    f32 = jnp.float32
    qf = jax.nn.silu(q.astype(f32))
    fs = lb + (1.0 - lb) * jax.nn.sigmoid(f_logit.astype(f32))
    logf = jnp.log(fs)
    kf = 1.0 - fs
    vf = i.astype(f32)

    def blocks(z, d):
        return z.reshape(B, N, CHUNK, HG_HEADS, d).transpose(0, 3, 1, 2, 4)

    qb, kb, lb_ = blocks(qf, HG_DK), blocks(kf, HG_DK), blocks(logf, HG_DK)
    vb = blocks(vf, HG_DV)
    b = jnp.cumsum(lb_, axis=3)
    b_ref = b[:, :, :, CHUNK // 2:CHUNK // 2 + 1]
    b_last = b[:, :, :, -1:]
    scores = jnp.einsum('bhnik,bhnjk->bhnij', qb * jnp.exp(b - b_ref), kb * jnp.exp(b_ref - b))
    mask = jnp.tril(jnp.ones((CHUNK, CHUNK), dtype=bool))
    scores = jnp.where(mask, scores, 0.0)
    o = jnp.einsum('bhnij,bhnjv->bhniv', scores, vb)
    U = jnp.einsum('bhnck,bhncv->bhnkv', kb * jnp.exp(b_last - b), vb)
    chunk_decay = jnp.exp(b_last[:, :, :, 0])

    def step(S, inp):
        dec, u = inp
        return dec[..., None] * S + u, S

    S0 = jnp.zeros((B, HG_HEADS, HG_DK, HG_DV), f32)
    _, S_prev = lax.scan(step, S0, (jnp.moveaxis(chunk_decay, 2, 0), jnp.moveaxis(U, 2, 0)))
    S_prev = jnp.moveaxis(S_prev, 0, 2)
    o = o + jnp.einsum('bhnck,bhnkv->bhncv', qb * jnp.exp(b), S_prev)
    o = o * lax.rsqrt(jnp.mean(o * o, axis=-1, keepdims=True) + EPS)
    o = o.transpose(0, 2, 3, 1, 4).reshape(B, T, HG_WIDTH)
    o = o * norm_w.astype(f32) * jax.nn.silu(g.astype(f32))
    return o.astype(q.dtype)


def rwkv7_mixer(p, mix, w0, w2, a0, a2, g2, k_k, k_a, r_k, ln_w, ln_b):
    B, T, _ = p.shape
    f32 = jnp.float32
    dtype = p.dtype
    pf = p.astype(f32)
    pf = pf + (token_shift(pf) - pf) * mix.astype(f32)
    r, k, v, dw, da, dg = jnp.split(pf, [RW_WIDTH, 2 * RW_WIDTH, 3 * RW_WIDTH,
                                         3 * RW_WIDTH + RW_DECAY_LORA,
                                         3 * RW_WIDTH + RW_DECAY_LORA + RW_A_LORA], axis=-1)
    w = -jax.nn.softplus(-(w0.astype(f32) + jnp.tanh(dw) @ w2.astype(f32))) - 0.5
    decay = jnp.exp(-jnp.exp(w))
    a = jax.nn.sigmoid(a0.astype(f32) + da @ a2.astype(f32))
    g = jax.nn.sigmoid(dg) @ g2.astype(f32)

    def heads(z):
        return z.reshape(B, T, RW_HEADS, RW_HEAD)

    kk = heads(k * k_k.astype(f32))
    kk = kk / jnp.maximum(jnp.sqrt(jnp.sum(kk * kk, axis=-1, keepdims=True)), 1e-12)
    k = k * (1.0 + (a - 1.0) * k_a.astype(f32))
    rh, wh, kh, vh, ah = heads(r), heads(decay), heads(k), heads(v), heads(a)

    def step(S, inp):
        r_t, w_t, k_t, v_t, kk_t, a_t = inp
        sa = jnp.einsum('bhvk,bhk->bhv', S, -kk_t)
        S = S * w_t[:, :, None, :] + sa[..., None] * (kk_t * a_t)[:, :, None, :] \
            + v_t[..., None] * k_t[:, :, None, :]
        return S, jnp.einsum('bhvk,bhk->bhv', S, r_t)

    S0 = jnp.zeros((B, RW_HEADS, RW_HEAD, RW_HEAD), f32)
    xs = tuple(jnp.moveaxis(z, 1, 0) for z in (rh, wh, kh, vh, kk, ah))
    _, y = lax.scan(step, S0, xs)
    y = jnp.moveaxis(y, 0, 1)
    mu = jnp.mean(y, axis=-1, keepdims=True)
    var = jnp.mean(jnp.square(y - mu), axis=-1, keepdims=True)
    y = ((y - mu) * lax.rsqrt(var + RW_GN_EPS)).reshape(B, T, RW_WIDTH)
    y = y * ln_w.astype(f32) + ln_b.astype(f32)
    bonus = jnp.sum(rh * kh * r_k.astype(f32), axis=-1, keepdims=True) * vh
    y = (y + bonus.reshape(B, T, RW_WIDTH)) * g
    return y.astype(dtype)


def setup_inputs(seed: int = 0) -> dict:
    key = jax.random.key(seed)
    ks = jax.random.split(key, 24)
    f32 = jnp.float32
    nrm = lambda k, shape, s: jax.random.normal(k, shape, f32) * s
    return {
        'x': jax.random.normal(ks[0], (BATCH, SEQ, D_MODEL), f32),
        'norm1_w': 1.0 + nrm(ks[1], (DEPTH, D_MODEL), 0.05),
        'w_in': nrm(ks[2], (DEPTH, D_MODEL, IN_COLS), D_MODEL ** -0.5),
        'hg_lower_bounds': nrm(ks[3], (DEPTH + 1, HG_WIDTH), 0.1),
        'hg_norm_w': 1.0 + nrm(ks[4], (DEPTH, HG_WIDTH), 0.05),
        'rw_mix': jax.random.uniform(ks[5], (DEPTH, RW_COLS), f32),
        'rw_w0': jax.random.uniform(ks[6], (DEPTH, RW_WIDTH), f32, -5.0, -0.5),
        'rw_w2': nrm(ks[7], (DEPTH, RW_DECAY_LORA, RW_WIDTH), 0.1 * RW_DECAY_LORA ** -0.5),
        'rw_a0': nrm(ks[8], (DEPTH, RW_WIDTH), 0.5),
        'rw_a2': nrm(ks[9], (DEPTH, RW_A_LORA, RW_WIDTH), 0.5 * RW_A_LORA ** -0.5),
        'rw_g2': nrm(ks[10], (DEPTH, RW_GATE_LORA, RW_WIDTH), RW_GATE_LORA ** -0.5),
        'rw_k_k': 0.85 + nrm(ks[11], (DEPTH, RW_WIDTH), 0.05),
        'rw_k_a': 1.0 + nrm(ks[12], (DEPTH, RW_WIDTH), 0.05),
        'rw_r_k': nrm(ks[13], (DEPTH, RW_HEADS, RW_HEAD), 0.1),
        'rw_ln_w': 1.0 + nrm(ks[14], (DEPTH, RW_WIDTH), 0.05),
        'rw_ln_b': nrm(ks[15], (DEPTH, RW_WIDTH), 0.02),
        'w_out': nrm(ks[16], (DEPTH, D_MODEL, D_MODEL), D_MODEL ** -0.5),
        'norm2_w': 1.0 + nrm(ks[17], (DEPTH, D_MODEL), 0.05),
        'ffn_w1': nrm(ks[18], (DEPTH, D_MODEL, D_FF), D_MODEL ** -0.5),
        'ffn_w3': nrm(ks[19], (DEPTH, D_MODEL, D_FF), D_MODEL ** -0.5),
        'ffn_w2': nrm(ks[20], (DEPTH, D_FF, D_MODEL), D_FF ** -0.5),
        'final_norm_w': 1.0 + nrm(ks[21], (D_MODEL,), 0.05),
    }


def reference(x, norm1_w, w_in, hg_lower_bounds, hg_norm_w, rw_mix, rw_w0, rw_w2, rw_a0,
              rw_a2, rw_g2, rw_k_k, rw_k_a, rw_r_k, rw_ln_w, rw_ln_b, w_out, norm2_w,
              ffn_w1, ffn_w3, ffn_w2, final_norm_w):
    lbs = jnp.cumsum(jax.nn.softmax(hg_lower_bounds.astype(jnp.float32), axis=0), axis=0)
    for l in range(DEPTH):
        h = rmsnorm(x, norm1_w[l])
        proj = h @ w_in[l]
        q, f_logit, i, g = jnp.split(proj[..., :HG_COLS], 4, axis=-1)
        o_hg = hgrn2_mixer(q, f_logit, i, g, lbs[l], hg_norm_w[l])
        o_rw = rwkv7_mixer(proj[..., HG_COLS:], rw_mix[l], rw_w0[l], rw_w2[l], rw_a0[l], rw_a2[l],
                           rw_g2[l], rw_k_k[l], rw_k_a[l], rw_r_k[l], rw_ln_w[l], rw_ln_b[l])
        x = x + jnp.concatenate([o_hg, o_rw], axis=-1) @ w_out[l]
        h = rmsnorm(x, norm2_w[l])
        x = x + (jax.nn.silu(h @ ffn_w1[l]) * (h @ ffn_w3[l])) @ ffn_w2[l]
    return rmsnorm(x, final_norm_w)
```

```python
import math

import jax
import jax.numpy as jnp
from jax import lax
from jax.experimental import pallas as pl
from jax.experimental.pallas import tpu as pltpu

F32 = jnp.float32
BF16 = jnp.bfloat16

D_MODEL = 1024
CHUNK = 64
EPS = 1e-6
HG_WIDTH = 512
HG_DK = 128
HG_HEADS = 4
RW_WIDTH = 512
RW_HEAD = 64
RW_HEADS = 8
RW_GN_EPS = 64e-5
HG_COLS = 4 * HG_WIDTH
RW_COLS = 3 * RW_WIDTH + 64 + 64 + 128
D_FF = 2816
LANES = 128
PAIRS = RW_WIDTH // LANES
VMEM_LIMIT = 56 * 1024 * 1024


def _dot(a, b):
    return jnp.dot(a.astype(BF16), b.astype(BF16), preferred_element_type=F32)


def _dot_nt(a, b):
    return lax.dot_general(a.astype(BF16), b.astype(BF16), (((1,), (1,)), ((), ())),
                           preferred_element_type=F32)


def _dot_tn(a, b):
    return jnp.dot(a.T.astype(BF16), b.astype(BF16), preferred_element_type=F32)


def _cumsum_rows(tril, x):
    hi = x.astype(BF16)
    lo = (x - hi.astype(F32)).astype(BF16)
    return (jnp.dot(tril, hi, preferred_element_type=F32)
            + jnp.dot(tril, lo, preferred_element_type=F32))


def _sigmoid(x):
    return 1.0 / (1.0 + jnp.exp(-x))


def _silu(x):
    return x * _sigmoid(x)


def _proj_kernel(x_ref, nw_ref, w_ref, hg_ref, rw_ref, h_scr):
    x = x_ref[...]
    ms = jnp.mean(x * x, axis=-1, keepdims=True)
    h_scr[...] = (x * lax.rsqrt(ms + EPS) * nw_ref[...]).astype(BF16)
    step = 512
    for n in range(0, HG_COLS, step):
        hg_ref[:, n:n + step] = jnp.dot(h_scr[...], w_ref[:, n:n + step],
                                        preferred_element_type=F32)
    for n in range(0, RW_COLS, 256):
        rw_ref[:, n:n + 256] = jnp.dot(h_scr[...], w_ref[:, HG_COLS + n:HG_COLS + n + 256],
                                       preferred_element_type=F32)


def _proj(x2, norm_w, w_in_bf, tm):
    rows = x2.shape[0]
    in_cols = w_in_bf.shape[1]
    return pl.pallas_call(
        _proj_kernel,
        grid=(rows // tm,),
        in_specs=[
            pl.BlockSpec((tm, D_MODEL), lambda i: (i, 0)),
            pl.BlockSpec((1, D_MODEL), lambda i: (0, 0)),
            pl.BlockSpec((D_MODEL, in_cols), lambda i: (0, 0)),
        ],
        out_specs=[
            pl.BlockSpec((tm, HG_COLS), lambda i: (i, 0)),
            pl.BlockSpec((tm, RW_COLS), lambda i: (i, 0)),
        ],
        out_shape=[
            jax.ShapeDtypeStruct((rows, HG_COLS), F32),
            jax.ShapeDtypeStruct((rows, RW_COLS), F32),
        ],
        scratch_shapes=[pltpu.VMEM((tm, D_MODEL), BF16)],
        compiler_params=pltpu.CompilerParams(
            dimension_semantics=("parallel",), vmem_limit_bytes=VMEM_LIMIT),
        name="proj",
    )(x2, norm_w, w_in_bf)


def _hgrn2_kernel(p_ref, lbp_ref, nw_ref, tril_ref, o_ref, st_ref):
    @pl.when(pl.program_id(1) == 0)
    def _():
        st_ref[...] = jnp.zeros_like(st_ref)

    lbp = lbp_ref[...]
    m = jnp.maximum(lbp[0:1], lbp[1:2])
    e0 = jnp.exp(lbp[0:1] - m)
    e1 = jnp.exp(lbp[1:2] - m)
    lb = e0 / (e0 + e1)

    q = p_ref[:, 0:HG_WIDTH]
    f = p_ref[:, HG_WIDTH:2 * HG_WIDTH]
    v = p_ref[:, 2 * HG_WIDTH:3 * HG_WIDTH]
    g = p_ref[:, 3 * HG_WIDTH:4 * HG_WIDTH]
    qf = _silu(q)
    fs = lb + (1.0 - lb) * _sigmoid(f)
    logf = jnp.log(fs)
    kf = 1.0 - fs
    b = _cumsum_rows(tril_ref[...], logf)
    b_ref = b[CHUNK // 2:CHUNK // 2 + 1]
    b_last = b[CHUNK - 1:CHUNK]
    qs = qf * jnp.exp(b - b_ref)
    ks = kf * jnp.exp(b_ref - b)
    q0 = qf * jnp.exp(b)
    kh = kf * jnp.exp(b_last - b)
    dec = jnp.exp(b_last)

    row = lax.broadcasted_iota(jnp.int32, (CHUNK, CHUNK), 0)
    col = lax.broadcasted_iota(jnp.int32, (CHUNK, CHUNK), 1)
    causal = col <= row
    nw = nw_ref[...]
    gate = _silu(g)
    for h in range(HG_HEADS):
        sl = slice(h * HG_DK, (h + 1) * HG_DK)
        vh = v[:, sl]
        st = st_ref[h]
        scores = jnp.where(causal, _dot_nt(qs[:, sl], ks[:, sl]), 0.0)
        o = _dot(scores, vh) + _dot_nt(q0[:, sl], st)
        st_ref[h] = st * dec[:, sl] + _dot_tn(vh, kh[:, sl])
        o = o * lax.rsqrt(jnp.mean(o * o, axis=-1, keepdims=True) + EPS)
        o_ref[:, sl] = (o * nw[:, sl] * gate[:, sl]).astype(o_ref.dtype)


def _hgrn2(proj_hg, lbp, norm_w, tril, batch, seq):
    nchunk = seq // CHUNK
    return pl.pallas_call(
        _hgrn2_kernel,
        grid=(batch, nchunk),
        in_specs=[
            pl.BlockSpec((None, CHUNK, HG_COLS), lambda b, c: (b, c, 0)),
            pl.BlockSpec((2, HG_WIDTH), lambda b, c: (0, 0)),
            pl.BlockSpec((1, HG_WIDTH), lambda b, c: (0, 0)),
            pl.BlockSpec((CHUNK, CHUNK), lambda b, c: (0, 0)),
        ],
        out_specs=pl.BlockSpec((None, CHUNK, HG_WIDTH), lambda b, c: (b, c, 0)),
        out_shape=jax.ShapeDtypeStruct((batch, seq, HG_WIDTH), BF16),
        scratch_shapes=[pltpu.VMEM((HG_HEADS, HG_DK, HG_DK), F32)],
        compiler_params=pltpu.CompilerParams(
            dimension_semantics=("parallel", "arbitrary"), vmem_limit_bytes=VMEM_LIMIT),
        name="hgrn2",
    )(proj_hg, lbp, norm_w, tril)


def _rwkv7_kernel(p_ref, mix_ref, w0_ref, a0_ref, lora_ref, g2_ref, kk_ref, ka_ref, rk_ref,
                  lnw_ref, lnb_ref, tril_ref, seg_ref, o_ref, s_ref, prev_ref):
    first = pl.program_id(1) == 0

    @pl.when(first)
    def _():
        s_ref[...] = jnp.zeros_like(s_ref)
        prev_ref[...] = jnp.zeros_like(prev_ref)

    p = p_ref[...]
    rowi = lax.broadcasted_iota(jnp.int32, (CHUNK, RW_COLS), 0)
    shifted = jnp.where(rowi == 0, prev_ref[...], pltpu.roll(p, 1, axis=0))
    prev_ref[...] = p[CHUNK - 1:CHUNK]
    pf = p + (shifted - p) * mix_ref[...]

    r = pf[:, 0:RW_WIDTH]
    k = pf[:, RW_WIDTH:2 * RW_WIDTH]
    v = pf[:, 2 * RW_WIDTH:3 * RW_WIDTH]
    d_wa = pf[:, 3 * RW_WIDTH:3 * RW_WIDTH + LANES]
    d_g = pf[:, 3 * RW_WIDTH + LANES:3 * RW_WIDTH + 2 * LANES]
    lane = lax.broadcasted_iota(jnp.int32, (CHUNK, LANES), 1)
    m0 = lane < RW_HEAD
    lora = _dot(jnp.where(m0, jnp.tanh(d_wa), d_wa), lora_ref[...])
    u = w0_ref[...] + lora[:, 0:RW_WIDTH]
    ld = -math.exp(-0.5) * _sigmoid(u)
    alpha = _sigmoid(a0_ref[...] + lora[:, RW_WIDTH:2 * RW_WIDTH])
    gate = _dot(_sigmoid(d_g), g2_ref[...])

    seg = seg_ref[...]
    kk = k * kk_ref[...]
    n2 = _dot(kk * kk, seg)
    kk = kk * jnp.minimum(lax.rsqrt(n2), 1e12)
    km = k * (1.0 + (alpha - 1.0) * ka_ref[...])

    cum = _cumsum_rows(tril_ref[...], ld)
    cum_prev = cum - ld
    c_mid = cum[CHUNK // 2 - 1:CHUNK // 2]
    c_last = cum[CHUNK - 1:CHUNK]
    e_mid = jnp.exp(c_mid)
    e_tail = jnp.exp(c_last - c_mid)
    gamma = jnp.exp(c_last)
    dn = jnp.exp(c_mid - cum)
    ac = -kk * jnp.exp(cum_prev - c_mid)
    bc = kk * alpha * dn
    kc = km * dn
    rc = r * jnp.exp(cum - c_mid)
    a0 = ac * e_mid
    r0 = rc * e_mid
    bh = bc * e_tail
    kh = kc * e_tail

    t_i = lax.broadcasted_iota(jnp.int32, (CHUNK, LANES), 0)
    s_i = lane & (RW_HEAD - 1)
    strict = s_i < t_i
    incl = s_i <= t_i
    eye = (s_i == t_i).astype(F32)
    r128 = lax.broadcasted_iota(jnp.int32, (LANES, LANES), 0)
    c128 = lax.broadcasted_iota(jnp.int32, (LANES, LANES), 1)
    same_head = (r128 >= RW_HEAD) == (c128 >= RW_HEAD)
    diag128 = r128 == c128
    zeros = jnp.zeros((CHUNK, LANES), F32)

    def bd(x):
        return jnp.concatenate([jnp.where(m0, x, 0.0), jnp.where(m0, 0.0, x)], axis=0)

    def level_mask(j):
        return ((t_i >> (j + 1)) == (s_i >> (j + 1))) & ((t_i >> j) != (s_i >> j)) & strict

    ys = []
    for pr in range(PAIRS):
        sl = slice(pr * LANES, (pr + 1) * LANES)
        vp = v[:, sl]
        gram = _dot_nt(jnp.concatenate([ac[:, sl], rc[:, sl]], axis=0),
                       jnp.concatenate([bd(bc[:, sl]), bd(kc[:, sl])], axis=0))
        n_ab = jnp.where(strict, gram[0:CHUNK, 0:LANES], 0.0)
        m_ak = jnp.where(strict, gram[0:CHUNK, LANES:2 * LANES], 0.0)
        p_rb = jnp.where(incl, gram[CHUNK:2 * CHUNK, 0:LANES], 0.0)
        p_rk = jnp.where(incl, gram[CHUNK:2 * CHUNK, LANES:2 * LANES], 0.0)
        tinv = eye + jnp.where(level_mask(0), n_ab, 0.0)
        for j in range(1, 6):
            fj = _dot(jnp.where(level_mask(j), n_ab, 0.0), bd(tinv))
            tinv = tinv + _dot(tinv, bd(fj))
        x1 = _dot(m_ak, bd(vp))
        uw = _dot(tinv, jnp.concatenate([bd(x1), bd(a0[:, sl])], axis=1))
        u_loc = uw[:, 0:LANES]
        w_eff = uw[:, LANES:2 * LANES]
        rhs = jnp.concatenate([
            jnp.concatenate([bd(u_loc), bd(w_eff)], axis=1),
            jnp.concatenate([bd(vp), jnp.zeros((LANES, LANES), F32)], axis=1)], axis=0)
        yq = _dot(jnp.concatenate([p_rb, p_rk], axis=1), rhs)
        q_eff = r0[:, sl] + yq[:, LANES:2 * LANES]
        st = s_ref[pr]
        ys.append(_dot_nt(q_eff, bd(st)) + yq[:, 0:LANES])
        lhs_t = jnp.concatenate([
            jnp.concatenate([u_loc, w_eff], axis=1),
            jnp.concatenate([vp, zeros], axis=1)], axis=0)
        gd = _dot_tn(lhs_t, jnp.concatenate([bh[:, sl], kh[:, sl]], axis=0))
        d_t = jnp.where(m0, gd[0:CHUNK], gd[CHUNK:LANES])
        g_t = jnp.where(same_head, gd[LANES:2 * LANES], 0.0) + jnp.where(
            diag128, jnp.broadcast_to(gamma[:, sl], (LANES, LANES)), 0.0)
        s_ref[pr] = _dot(st, g_t) + d_t
    y = jnp.concatenate(ys, axis=1)

    inv_n = 1.0 / RW_HEAD
    mu = _dot(y, seg) * inv_n
    yc = y - mu
    var = _dot(yc * yc, seg) * inv_n
    yn = yc * lax.rsqrt(var + RW_GN_EPS) * lnw_ref[...] + lnb_ref[...]
    bonus = _dot(r * km * rk_ref[...], seg) * v
    o_ref[...] = ((yn + bonus) * gate).astype(o_ref.dtype)


def _rwkv7(proj_rw, mix, w0, a0, lora_w, g2, k_k, k_a, r_k, ln_w, ln_b, tril, seg, batch, seq):
    nchunk = seq // CHUNK
    vec = lambda n: pl.BlockSpec((1, n), lambda b, c: (0, 0))
    mat = lambda r, c_: pl.BlockSpec((r, c_), lambda b, c: (0, 0))
    return pl.pallas_call(
        _rwkv7_kernel,
        grid=(batch, nchunk),
        in_specs=[
            pl.BlockSpec((None, CHUNK, RW_COLS), lambda b, c: (b, c, 0)),
            vec(RW_COLS), vec(RW_WIDTH), vec(RW_WIDTH),
            mat(LANES, 2 * RW_WIDTH), mat(LANES, RW_WIDTH),
            vec(RW_WIDTH), vec(RW_WIDTH), vec(RW_WIDTH), vec(RW_WIDTH), vec(RW_WIDTH),
            mat(CHUNK, CHUNK), mat(RW_WIDTH, RW_WIDTH),
        ],
        out_specs=pl.BlockSpec((None, CHUNK, RW_WIDTH), lambda b, c: (b, c, 0)),
        out_shape=jax.ShapeDtypeStruct((batch, seq, RW_WIDTH), BF16),
        scratch_shapes=[pltpu.VMEM((PAIRS, RW_HEAD, LANES), F32),
                        pltpu.VMEM((1, RW_COLS), F32)],
        compiler_params=pltpu.CompilerParams(
            dimension_semantics=("parallel", "arbitrary"), vmem_limit_bytes=VMEM_LIMIT),
        name="rwkv7",
    )(proj_rw, mix, w0, a0, lora_w, g2, k_k, k_a, r_k, ln_w, ln_b, tril, seg)


def _ffn_kernel(x_ref, ohg_ref, orw_ref, wo_ref, n2_ref, w1_ref, w3_ref, w2_ref, fn_ref,
                out_ref):
    x1 = (x_ref[...]
          + jnp.dot(ohg_ref[...], wo_ref[0:HG_WIDTH, :], preferred_element_type=F32)
          + jnp.dot(orw_ref[...], wo_ref[HG_WIDTH:D_MODEL, :], preferred_element_type=F32))
    ms = jnp.mean(x1 * x1, axis=-1, keepdims=True)
    h = (x1 * lax.rsqrt(ms + EPS) * n2_ref[...]).astype(BF16)
    a = jnp.dot(h, w1_ref[...], preferred_element_type=F32)
    b = jnp.dot(h, w3_ref[...], preferred_element_type=F32)
    hm = (_silu(a) * b).astype(BF16)
    x2 = x1 + jnp.dot(hm, w2_ref[...], preferred_element_type=F32)
    ms2 = jnp.mean(x2 * x2, axis=-1, keepdims=True)
    out_ref[...] = x2 * lax.rsqrt(ms2 + EPS) * fn_ref[...]


def _ffn(x2, o_hg, o_rw, w_out, norm2_w, w1, w3, w2, final_w, tm):
    rows = x2.shape[0]
    const = lambda r, c_: pl.BlockSpec((r, c_), lambda i: (0, 0))
    return pl.pallas_call(
        _ffn_kernel,
        grid=(rows // tm,),
        in_specs=[
            pl.BlockSpec((tm, D_MODEL), lambda i: (i, 0)),
            pl.BlockSpec((tm, HG_WIDTH), lambda i: (i, 0)),
            pl.BlockSpec((tm, RW_WIDTH), lambda i: (i, 0)),
            const(D_MODEL, D_MODEL), const(1, D_MODEL),
            const(D_MODEL, D_FF), const(D_MODEL, D_FF), const(D_FF, D_MODEL),
            const(1, D_MODEL),
        ],
        out_specs=pl.BlockSpec((tm, D_MODEL), lambda i: (i, 0)),
        out_shape=jax.ShapeDtypeStruct((rows, D_MODEL), F32),
        compiler_params=pltpu.CompilerParams(
            dimension_semantics=("parallel",), vmem_limit_bytes=VMEM_LIMIT),
        name="outproj_ffn",
    )(x2, o_hg, o_rw, w_out, norm2_w, w1, w3, w2, final_w)


def kernel(x, norm1_w, w_in, hg_lower_bounds, hg_norm_w, rw_mix, rw_w0, rw_w2, rw_a0, rw_a2,
           rw_g2, rw_k_k, rw_k_a, rw_r_k, rw_ln_w, rw_ln_b, w_out, norm2_w, ffn_w1, ffn_w3,
           ffn_w2, final_norm_w):
    batch, seq, _ = x.shape
    rows = batch * seq
    x2 = x.reshape(rows, D_MODEL)
    row = lambda z: z.reshape(1, -1)

    proj_hg, proj_rw = _proj(x2, row(norm1_w[0]), w_in[0].astype(BF16), tm=512)

    tril = jnp.tril(jnp.ones((CHUNK, CHUNK), BF16))
    o_hg = _hgrn2(proj_hg.reshape(batch, seq, HG_COLS), hg_lower_bounds, row(hg_norm_w[0]),
                  tril, batch, seq)

    zero = jnp.zeros((64, RW_WIDTH), F32)
    lora_w = jnp.concatenate([jnp.concatenate([rw_w2[0], zero], axis=1),
                              jnp.concatenate([zero, rw_a2[0]], axis=1)], axis=0).astype(BF16)
    head_id = jnp.arange(RW_WIDTH) // RW_HEAD
    seg = (head_id[:, None] == head_id[None, :]).astype(BF16)
    o_rw = _rwkv7(proj_rw.reshape(batch, seq, RW_COLS), row(rw_mix[0]), row(rw_w0[0]),
                  row(rw_a0[0]), lora_w, rw_g2[0].astype(BF16), row(rw_k_k[0]), row(rw_k_a[0]),
                  row(rw_r_k[0]), row(rw_ln_w[0]), row(rw_ln_b[0]), tril, seg, batch, seq)

    out = _ffn(x2, o_hg.reshape(rows, HG_WIDTH), o_rw.reshape(rows, RW_WIDTH),
               w_out[0].astype(BF16), row(norm2_w[0]), ffn_w1[0].astype(BF16),
               ffn_w3[0].astype(BF16), ffn_w2[0].astype(BF16), row(final_norm_w), tm=256)
    return out.reshape(batch, seq, D_MODEL)
```

```python
import math

import jax
import jax.numpy as jnp
from jax import lax
from jax.experimental import pallas as pl
from jax.experimental.pallas import tpu as pltpu

F32 = jnp.float32
BF16 = jnp.bfloat16

D_MODEL = 1024
CHUNK = 64
EPS = 1e-6
HG_WIDTH = 512
HG_DK = 128
HG_HEADS = 4
RW_WIDTH = 512
RW_HEAD = 64
RW_HEADS = 8
RW_GN_EPS = 64e-5
HG_COLS = 4 * HG_WIDTH
RW_COLS = 3 * RW_WIDTH + 64 + 64 + 128
D_FF = 2816
LANES = 128
PAIRS = RW_WIDTH // LANES
MIX_BLOCK_CHUNKS = 4
VMEM_LIMIT = 56 * 1024 * 1024


def _dot(a, b):
    return jnp.dot(a.astype(BF16), b.astype(BF16), preferred_element_type=F32)


def _dot_nt(a, b):
    return lax.dot_general(a.astype(BF16), b.astype(BF16), (((1,), (1,)), ((), ())),
                           preferred_element_type=F32)


def _dot_tn(a, b):
    return jnp.dot(a.T.astype(BF16), b.astype(BF16), preferred_element_type=F32)


def _cumsum_rows(tril, x):
    hi = x.astype(BF16)
    lo = (x - hi.astype(F32)).astype(BF16)
    return (jnp.dot(tril, hi, preferred_element_type=F32)
            + jnp.dot(tril, lo, preferred_element_type=F32))


def _sigmoid(x):
    return 1.0 / (1.0 + jnp.exp(-x))


def _silu(x):
    return x * _sigmoid(x)


def _proj_kernel(x_ref, nw_ref, w_ref, hg_ref, rw_ref, h_scr):
    x = x_ref[...]
    ms = jnp.mean(x * x, axis=-1, keepdims=True)
    h_scr[...] = (x * lax.rsqrt(ms + EPS) * nw_ref[...]).astype(BF16)
    step = 512
    for n in range(0, HG_COLS, step):
        hg_ref[:, n:n + step] = jnp.dot(h_scr[...], w_ref[:, n:n + step],
                                        preferred_element_type=F32)
    for n in range(0, RW_COLS, 256):
        rw_ref[:, n:n + 256] = jnp.dot(h_scr[...], w_ref[:, HG_COLS + n:HG_COLS + n + 256],
                                       preferred_element_type=F32)


def _proj(x2, norm_w, w_in_bf, tm):
    rows = x2.shape[0]
    in_cols = w_in_bf.shape[1]
    return pl.pallas_call(
        _proj_kernel,
        grid=(rows // tm,),
        in_specs=[
            pl.BlockSpec((tm, D_MODEL), lambda i: (i, 0)),
            pl.BlockSpec((1, D_MODEL), lambda i: (0, 0)),
            pl.BlockSpec((D_MODEL, in_cols), lambda i: (0, 0)),
        ],
        out_specs=[
            pl.BlockSpec((tm, HG_COLS), lambda i: (i, 0)),
            pl.BlockSpec((tm, RW_COLS), lambda i: (i, 0)),
        ],
        out_shape=[
            jax.ShapeDtypeStruct((rows, HG_COLS), F32),
            jax.ShapeDtypeStruct((rows, RW_COLS), F32),
        ],
        scratch_shapes=[pltpu.VMEM((tm, D_MODEL), BF16)],
        compiler_params=pltpu.CompilerParams(
            dimension_semantics=("parallel",), vmem_limit_bytes=VMEM_LIMIT),
        name="proj",
    )(x2, norm_w, w_in_bf)


def _hgrn2_kernel(p_ref, lbp_ref, nw_ref, tril_ref, o_ref, st_ref):
    nchunk = p_ref.shape[0] // CHUNK

    @pl.when(pl.program_id(1) == 0)
    def _():
        st_ref[...] = jnp.zeros_like(st_ref)

    lbp = lbp_ref[...]
    m = jnp.maximum(lbp[0:1], lbp[1:2])
    e0 = jnp.exp(lbp[0:1] - m)
    e1 = jnp.exp(lbp[1:2] - m)
    lb = e0 / (e0 + e1)

    q = p_ref[:, 0:HG_WIDTH]
    f = p_ref[:, HG_WIDTH:2 * HG_WIDTH]
    v = p_ref[:, 2 * HG_WIDTH:3 * HG_WIDTH]
    g = p_ref[:, 3 * HG_WIDTH:4 * HG_WIDTH]
    qf = _silu(q)
    fs = lb + (1.0 - lb) * _sigmoid(f)
    kf = 1.0 - fs
    b = _cumsum_rows(tril_ref[...], jnp.log(fs))
    nw = nw_ref[...]
    gate = _silu(g) * nw

    row = lax.broadcasted_iota(jnp.int32, (CHUNK, CHUNK), 0)
    col = lax.broadcasted_iota(jnp.int32, (CHUNK, CHUNK), 1)
    causal = col <= row

    qs, ks, q0, kh, vv, dec = ([] for _ in range(6))
    for c in range(nchunk):
        rs = slice(c * CHUNK, (c + 1) * CHUNK)
        b_c = b[rs]
        b_mid = b_c[CHUNK // 2:CHUNK // 2 + 1]
        b_last = b_c[CHUNK - 1:CHUNK]
        qs_c = qf[rs] * jnp.exp(b_c - b_mid)
        ks_c = kf[rs] * jnp.exp(b_mid - b_c)
        q0_c = qf[rs] * jnp.exp(b_c)
        kh_c = kf[rs] * jnp.exp(b_last - b_c)
        dec_c = jnp.exp(b_last)
        v_c = v[rs]
        for h in range(HG_HEADS):
            sl = slice(h * HG_DK, (h + 1) * HG_DK)
            for lst, val in ((qs, qs_c), (ks, ks_c), (q0, q0_c), (kh, kh_c), (vv, v_c),
                             (dec, dec_c)):
                lst.append(val[:, sl])
    items = range(nchunk * HG_HEADS)
    scores = [jnp.where(causal, _dot_nt(qs[i], ks[i]), 0.0) for i in items]
    kv = [_dot_tn(vv[i], kh[i]) for i in items]
    o_in = [_dot(scores[i], vv[i]) for i in items]
    st = [st_ref[h] for h in range(HG_HEADS)]
    o_rows = []
    for c in range(nchunk):
        o_c = []
        for h in range(HG_HEADS):
            i = c * HG_HEADS + h
            o = o_in[i] + _dot_nt(q0[i], st[h])
            st[h] = st[h] * dec[i] + kv[i]
            o_c.append(o * lax.rsqrt(jnp.mean(o * o, axis=-1, keepdims=True) + EPS))
        o_rows.append(jnp.concatenate(o_c, axis=1))
    for h in range(HG_HEADS):
        st_ref[h] = st[h]
    o_ref[...] = (jnp.concatenate(o_rows, axis=0) * gate).astype(o_ref.dtype)


def _hgrn2(proj_hg, lbp, norm_w, tril, batch, seq):
    rows = tril.shape[0]
    return pl.pallas_call(
        _hgrn2_kernel,
        grid=(batch, seq // rows),
        in_specs=[
            pl.BlockSpec((None, rows, HG_COLS), lambda b, c: (b, c, 0)),
            pl.BlockSpec((2, HG_WIDTH), lambda b, c: (0, 0)),
            pl.BlockSpec((1, HG_WIDTH), lambda b, c: (0, 0)),
            pl.BlockSpec((rows, rows), lambda b, c: (0, 0)),
        ],
        out_specs=pl.BlockSpec((None, rows, HG_WIDTH), lambda b, c: (b, c, 0)),
        out_shape=jax.ShapeDtypeStruct((batch, seq, HG_WIDTH), BF16),
        scratch_shapes=[pltpu.VMEM((HG_HEADS, HG_DK, HG_DK), F32)],
        compiler_params=pltpu.CompilerParams(
            dimension_semantics=("parallel", "arbitrary"), vmem_limit_bytes=VMEM_LIMIT),
        name="hgrn2",
    )(proj_hg, lbp, norm_w, tril)


def _rwkv7_kernel(p_ref, mix_ref, w0_ref, a0_ref, lora_ref, g2_ref, kk_ref, ka_ref, rk_ref,
                  lnw_ref, lnb_ref, tril_ref, seg_ref, o_ref, s_ref, prev_ref):
    rows = p_ref.shape[0]
    nchunk = rows // CHUNK

    @pl.when(pl.program_id(1) == 0)
    def _():
        s_ref[...] = jnp.zeros_like(s_ref)
        prev_ref[...] = jnp.zeros_like(prev_ref)

    p = p_ref[...]
    rowi = lax.broadcasted_iota(jnp.int32, (rows, RW_COLS), 0)
    shifted = jnp.where(rowi == 0, prev_ref[...], pltpu.roll(p, 1, axis=0))
    prev_ref[...] = p[rows - 1:rows]
    pf = p + (shifted - p) * mix_ref[...]

    r = pf[:, 0:RW_WIDTH]
    k = pf[:, RW_WIDTH:2 * RW_WIDTH]
    v = pf[:, 2 * RW_WIDTH:3 * RW_WIDTH]
    d_wa = pf[:, 3 * RW_WIDTH:3 * RW_WIDTH + LANES]
    d_g = pf[:, 3 * RW_WIDTH + LANES:3 * RW_WIDTH + 2 * LANES]
    lane_r = lax.broadcasted_iota(jnp.int32, (rows, LANES), 1)
    lora = _dot(jnp.where(lane_r < RW_HEAD, jnp.tanh(d_wa), d_wa), lora_ref[...])
    u = w0_ref[...] + lora[:, 0:RW_WIDTH]
    ld = -math.exp(-0.5) * _sigmoid(u)
    alpha = _sigmoid(a0_ref[...] + lora[:, RW_WIDTH:2 * RW_WIDTH])
    gate = _dot(_sigmoid(d_g), g2_ref[...])

    seg = seg_ref[...]
    kk = k * kk_ref[...]
    n2 = _dot(kk * kk, seg)
    kk = kk * jnp.minimum(lax.rsqrt(n2), 1e12)
    km = k * (1.0 + (alpha - 1.0) * ka_ref[...])
    cum = _cumsum_rows(tril_ref[...], ld)
    cum_prev = cum - ld
    kka = kk * alpha

    lane = lax.broadcasted_iota(jnp.int32, (CHUNK, LANES), 1)
    m0 = lane < RW_HEAD
    t_i = lax.broadcasted_iota(jnp.int32, (CHUNK, LANES), 0)
    s_i = lane & (RW_HEAD - 1)
    strict = s_i < t_i
    incl = s_i <= t_i
    eye = (s_i == t_i).astype(F32)
    levels = [((t_i >> (j + 1)) == (s_i >> (j + 1))) & ((t_i >> j) != (s_i >> j)) & strict
              for j in range(6)]
    r128 = lax.broadcasted_iota(jnp.int32, (LANES, LANES), 0)
    c128 = lax.broadcasted_iota(jnp.int32, (LANES, LANES), 1)
    same_head = (r128 >= RW_HEAD) == (c128 >= RW_HEAD)
    diag128 = r128 == c128
    zeros = jnp.zeros((CHUNK, LANES), F32)
    zeros128 = jnp.zeros((LANES, LANES), F32)

    def bd(x):
        return jnp.concatenate([jnp.where(m0, x, 0.0), jnp.where(m0, 0.0, x)], axis=0)

    ac, bc, kc, rc, a0, r0, bh, kh, vv, gam = ([] for _ in range(10))
    for c in range(nchunk):
        rs = slice(c * CHUNK, (c + 1) * CHUNK)
        cum_c = cum[rs]
        c_mid = cum_c[CHUNK // 2 - 1:CHUNK // 2]
        c_last = cum_c[CHUNK - 1:CHUNK]
        e_mid = jnp.exp(c_mid)
        e_tail = jnp.exp(c_last - c_mid)
        gamma = jnp.exp(c_last)
        dn = jnp.exp(c_mid - cum_c)
        ac_c = -kk[rs] * jnp.exp(cum_prev[rs] - c_mid)
        bc_c = kka[rs] * dn
        kc_c = km[rs] * dn
        rc_c = r[rs] * jnp.exp(cum_c - c_mid)
        a0_c = ac_c * e_mid
        r0_c = rc_c * e_mid
        bh_c = bc_c * e_tail
        kh_c = kc_c * e_tail
        v_c = v[rs]
        for pr in range(PAIRS):
            sl = slice(pr * LANES, (pr + 1) * LANES)
            for lst, val in ((ac, ac_c), (bc, bc_c), (kc, kc_c), (rc, rc_c), (a0, a0_c), (r0, r0_c),
                             (bh, bh_c), (kh, kh_c), (vv, v_c), (gam, gamma)):
                lst.append(val[:, sl])
    chains = range(nchunk * PAIRS)

    gram = [_dot_nt(jnp.concatenate([ac[i], rc[i]], axis=0),
                    jnp.concatenate([bd(bc[i]), bd(kc[i])], axis=0)) for i in chains]
    n_ab = [jnp.where(strict, gram[i][0:CHUNK, 0:LANES], 0.0) for i in chains]
    m_ak = [jnp.where(strict, gram[i][0:CHUNK, LANES:2 * LANES], 0.0) for i in chains]
    p_rb = [jnp.where(incl, gram[i][CHUNK:2 * CHUNK, 0:LANES], 0.0) for i in chains]
    p_rk = [jnp.where(incl, gram[i][CHUNK:2 * CHUNK, LANES:2 * LANES], 0.0) for i in chains]
    x1 = [_dot(m_ak[i], bd(vv[i])) for i in chains]
    tinv = [eye + jnp.where(levels[0], n_ab[i], 0.0) for i in chains]
    for j in range(1, 6):
        fj = [_dot(jnp.where(levels[j], n_ab[i], 0.0), bd(tinv[i])) for i in chains]
        tinv = [tinv[i] + _dot(tinv[i], bd(fj[i])) for i in chains]
    uw = [_dot(tinv[i], jnp.concatenate([bd(x1[i]), bd(a0[i])], axis=1)) for i in chains]
    u_loc = [uw[i][:, 0:LANES] for i in chains]
    w_eff = [uw[i][:, LANES:2 * LANES] for i in chains]
    yq = [_dot(jnp.concatenate([p_rb[i], p_rk[i]], axis=1),
               jnp.concatenate([jnp.concatenate([bd(u_loc[i]), bd(w_eff[i])], axis=1),
                                jnp.concatenate([bd(vv[i]), zeros128], axis=1)], axis=0))
          for i in chains]
    gd = [_dot_tn(jnp.concatenate([jnp.concatenate([u_loc[i], w_eff[i]], axis=1),
                                   jnp.concatenate([vv[i], zeros], axis=1)], axis=0),
                  jnp.concatenate([bh[i], kh[i]], axis=0)) for i in chains]
    d_t = [jnp.where(m0, gd[i][0:CHUNK], gd[i][CHUNK:LANES]) for i in chains]
    g_t = [jnp.where(same_head, gd[i][LANES:2 * LANES], 0.0)
           + jnp.where(diag128, jnp.broadcast_to(gam[i], (LANES, LANES)), 0.0) for i in chains]
    q_eff = [r0[i] + yq[i][:, LANES:2 * LANES] for i in chains]

    st = [s_ref[pr] for pr in range(PAIRS)]
    y_rows = []
    for c in range(nchunk):
        st_in = st
        st = [_dot(st_in[pr], g_t[c * PAIRS + pr]) + d_t[c * PAIRS + pr] for pr in range(PAIRS)]
        y_rows.append(jnp.concatenate(
            [_dot_nt(q_eff[c * PAIRS + pr], bd(st_in[pr])) + yq[c * PAIRS + pr][:, 0:LANES]
             for pr in range(PAIRS)], axis=1))
    for pr in range(PAIRS):
        s_ref[pr] = st[pr]
    y = jnp.concatenate(y_rows, axis=0)

    inv_n = 1.0 / RW_HEAD
    mu = _dot(y, seg) * inv_n
    yc = y - mu
    var = _dot(yc * yc, seg) * inv_n
    yn = yc * lax.rsqrt(var + RW_GN_EPS) * lnw_ref[...] + lnb_ref[...]
    bonus = _dot(r * km * rk_ref[...], seg) * v
    o_ref[...] = ((yn + bonus) * gate).astype(o_ref.dtype)


def _rwkv7(proj_rw, mix, w0, a0, lora_w, g2, k_k, k_a, r_k, ln_w, ln_b, tril, seg, batch, seq):
    rows = tril.shape[0]
    vec = lambda n: pl.BlockSpec((1, n), lambda b, c: (0, 0))
    mat = lambda r, c_: pl.BlockSpec((r, c_), lambda b, c: (0, 0))
    return pl.pallas_call(
        _rwkv7_kernel,
        grid=(batch, seq // rows),
        in_specs=[
            pl.BlockSpec((None, rows, RW_COLS), lambda b, c: (b, c, 0)),
            vec(RW_COLS), vec(RW_WIDTH), vec(RW_WIDTH),
            mat(LANES, 2 * RW_WIDTH), mat(LANES, RW_WIDTH),
            vec(RW_WIDTH), vec(RW_WIDTH), vec(RW_WIDTH), vec(RW_WIDTH), vec(RW_WIDTH),
            mat(rows, rows), mat(RW_WIDTH, RW_WIDTH),
        ],
        out_specs=pl.BlockSpec((None, rows, RW_WIDTH), lambda b, c: (b, c, 0)),
        out_shape=jax.ShapeDtypeStruct((batch, seq, RW_WIDTH), BF16),
        scratch_shapes=[pltpu.VMEM((PAIRS, RW_HEAD, LANES), F32),
                        pltpu.VMEM((1, RW_COLS), F32)],
        compiler_params=pltpu.CompilerParams(
            dimension_semantics=("parallel", "arbitrary"), vmem_limit_bytes=VMEM_LIMIT),
        name="rwkv7",
    )(proj_rw, mix, w0, a0, lora_w, g2, k_k, k_a, r_k, ln_w, ln_b, tril, seg)


def _ffn_kernel(x_ref, ohg_ref, orw_ref, wo_ref, n2_ref, w1_ref, w3_ref, w2_ref, fn_ref,
                out_ref):
    x1 = (x_ref[...]
          + jnp.dot(ohg_ref[...], wo_ref[0:HG_WIDTH, :], preferred_element_type=F32)
          + jnp.dot(orw_ref[...], wo_ref[HG_WIDTH:D_MODEL, :], preferred_element_type=F32))
    ms = jnp.mean(x1 * x1, axis=-1, keepdims=True)
    h = (x1 * lax.rsqrt(ms + EPS) * n2_ref[...]).astype(BF16)
    a = jnp.dot(h, w1_ref[...], preferred_element_type=F32)
    b = jnp.dot(h, w3_ref[...], preferred_element_type=F32)
    hm = (_silu(a) * b).astype(BF16)
    x2 = x1 + jnp.dot(hm, w2_ref[...], preferred_element_type=F32)
    ms2 = jnp.mean(x2 * x2, axis=-1, keepdims=True)
    out_ref[...] = x2 * lax.rsqrt(ms2 + EPS) * fn_ref[...]


def _ffn(x2, o_hg, o_rw, w_out, norm2_w, w1, w3, w2, final_w, tm):
    rows = x2.shape[0]
    const = lambda r, c_: pl.BlockSpec((r, c_), lambda i: (0, 0))
    return pl.pallas_call(
        _ffn_kernel,
        grid=(rows // tm,),
        in_specs=[
            pl.BlockSpec((tm, D_MODEL), lambda i: (i, 0)),
            pl.BlockSpec((tm, HG_WIDTH), lambda i: (i, 0)),
            pl.BlockSpec((tm, RW_WIDTH), lambda i: (i, 0)),
            const(D_MODEL, D_MODEL), const(1, D_MODEL),
            const(D_MODEL, D_FF), const(D_MODEL, D_FF), const(D_FF, D_MODEL),
            const(1, D_MODEL),
        ],
        out_specs=pl.BlockSpec((tm, D_MODEL), lambda i: (i, 0)),
        out_shape=jax.ShapeDtypeStruct((rows, D_MODEL), F32),
        compiler_params=pltpu.CompilerParams(
            dimension_semantics=("parallel",), vmem_limit_bytes=VMEM_LIMIT),
        name="outproj_ffn",
    )(x2, o_hg, o_rw, w_out, norm2_w, w1, w3, w2, final_w)


def kernel(x, norm1_w, w_in, hg_lower_bounds, hg_norm_w, rw_mix, rw_w0, rw_w2, rw_a0, rw_a2,
           rw_g2, rw_k_k, rw_k_a, rw_r_k, rw_ln_w, rw_ln_b, w_out, norm2_w, ffn_w1, ffn_w3,
           ffn_w2, final_norm_w):
    batch, seq, _ = x.shape
    rows = batch * seq
    x2 = x.reshape(rows, D_MODEL)
    row = lambda z: z.reshape(1, -1)

    proj_hg, proj_rw = _proj(x2, row(norm1_w[0]), w_in[0].astype(BF16), tm=512)

    tril = jnp.kron(jnp.eye(MIX_BLOCK_CHUNKS, dtype=BF16), jnp.tril(jnp.ones((CHUNK, CHUNK), BF16)))
    o_hg = _hgrn2(proj_hg.reshape(batch, seq, HG_COLS), hg_lower_bounds, row(hg_norm_w[0]),
                  tril, batch, seq)

    zero = jnp.zeros((64, RW_WIDTH), F32)
    lora_w = jnp.concatenate([jnp.concatenate([rw_w2[0], zero], axis=1),
                              jnp.concatenate([zero, rw_a2[0]], axis=1)], axis=0).astype(BF16)
    head_id = jnp.arange(RW_WIDTH) // RW_HEAD
    seg = (head_id[:, None] == head_id[None, :]).astype(BF16)
    o_rw = _rwkv7(proj_rw.reshape(batch, seq, RW_COLS), row(rw_mix[0]), row(rw_w0[0]),
                  row(rw_a0[0]), lora_w, rw_g2[0].astype(BF16), row(rw_k_k[0]), row(rw_k_a[0]),
                  row(rw_r_k[0]), row(rw_ln_w[0]), row(rw_ln_b[0]), tril, seg, batch, seq)

    out = _ffn(x2, o_hg.reshape(rows, HG_WIDTH), o_rw.reshape(rows, RW_WIDTH),
               w_out[0].astype(BF16), row(norm2_w[0]), ffn_w1[0].astype(BF16),
               ffn_w3[0].astype(BF16), ffn_w2[0].astype(BF16), row(final_norm_w), tm=256)
    return out.reshape(batch, seq, D_MODEL)
```

```python
import math

import jax
import jax.numpy as jnp
from jax import lax
from jax.experimental import pallas as pl
from jax.experimental.pallas import tpu as pltpu

F32 = jnp.float32
BF16 = jnp.bfloat16

D_MODEL = 1024
CHUNK = 64
EPS = 1e-6
HG_WIDTH = 512
HG_DK = 128
HG_HEADS = 4
RW_WIDTH = 512
RW_HEAD = 64
RW_HEADS = 8
RW_GN_EPS = 64e-5
HG_COLS = 4 * HG_WIDTH
RW_COLS = 3 * RW_WIDTH + 64 + 64 + 128
IN_COLS = HG_COLS + RW_COLS
D_FF = 2816
LANES = 128
MXU_WIDTH = 256
PAIRS = RW_WIDTH // LANES
MIX_BLOCK_CHUNKS = 4
VMEM_LIMIT = 56 * 1024 * 1024


def _dot(a, b):
    return jnp.dot(a.astype(BF16), b.astype(BF16), preferred_element_type=F32)


def _dot_nt(a, b):
    return lax.dot_general(a.astype(BF16), b.astype(BF16), (((1,), (1,)), ((), ())),
                           preferred_element_type=F32)


def _dot_tn(a, b):
    return jnp.dot(a.T.astype(BF16), b.astype(BF16), preferred_element_type=F32)


def _cumsum_rows(tril, x):
    hi = x.astype(BF16)
    lo = (x - hi.astype(F32)).astype(BF16)
    return (jnp.dot(tril, hi, preferred_element_type=F32)
            + jnp.dot(tril, lo, preferred_element_type=F32))


def _head_sums(x, seg):
    w = seg.shape[0]
    return jnp.concatenate([_dot(x[:, n:n + w], seg) for n in range(0, x.shape[1], w)], axis=1)


def _sigmoid(x):
    return 1.0 / (1.0 + jnp.exp(-x))


def _silu(x):
    return x * _sigmoid(x)


def _rmsnorm_bf16(x, w):
    ms = jnp.mean(x * x, axis=-1, keepdims=True)
    return (x * lax.rsqrt(ms + EPS) * w).astype(BF16)


def _hgrn2_block(p_ref, lbp_ref, nw_ref, tril, st_ref):
    nchunk = p_ref.shape[0] // CHUNK
    lbp = lbp_ref[...]
    m = jnp.maximum(lbp[0:1], lbp[1:2])
    e0 = jnp.exp(lbp[0:1] - m)
    e1 = jnp.exp(lbp[1:2] - m)
    lb = e0 / (e0 + e1)

    q = p_ref[:, 0:HG_WIDTH]
    f = p_ref[:, HG_WIDTH:2 * HG_WIDTH]
    v = p_ref[:, 2 * HG_WIDTH:3 * HG_WIDTH]
    g = p_ref[:, 3 * HG_WIDTH:4 * HG_WIDTH]
    qf = _silu(q)
    fs = lb + (1.0 - lb) * _sigmoid(f)
    kf = 1.0 - fs
    b = _cumsum_rows(tril, jnp.log(fs))
    gate = _silu(g) * nw_ref[...]

    row = lax.broadcasted_iota(jnp.int32, (CHUNK, CHUNK), 0)
    col = lax.broadcasted_iota(jnp.int32, (CHUNK, CHUNK), 1)
    causal = col <= row

    qs, ks, q0, kh, vv, dec = ([] for _ in range(6))
    for c in range(nchunk):
        rs = slice(c * CHUNK, (c + 1) * CHUNK)
        b_c = b[rs]
        b_mid = b_c[CHUNK // 2:CHUNK // 2 + 1]
        b_last = b_c[CHUNK - 1:CHUNK]
        qs_c = qf[rs] * jnp.exp(b_c - b_mid)
        ks_c = kf[rs] * jnp.exp(b_mid - b_c)
        q0_c = qs_c * jnp.exp(b_mid)
        kh_c = ks_c * jnp.exp(b_last - b_mid)
        dec_c = jnp.exp(b_last)
        v_c = v[rs]
        for h in range(HG_HEADS):
            sl = slice(h * HG_DK, (h + 1) * HG_DK)
            for lst, val in ((qs, qs_c), (ks, ks_c), (q0, q0_c), (kh, kh_c), (vv, v_c),
                             (dec, dec_c)):
                lst.append(val[:, sl])
    items = range(nchunk * HG_HEADS)
    yield
    scores =[jnp.where(causal, _dot_nt(qs[i], ks[i]), 0.0) for i in items]
    kv = [_dot_tn(vv[i], kh[i]) for i in items]
    o_in = [_dot(scores[i], vv[i]) for i in items]
    st = [st_ref[h] for h in range(HG_HEADS)]
    o_rows = []
    for c in range(nchunk):
        o_c = []
        for h in range(HG_HEADS):
            i = c * HG_HEADS + h
            o = o_in[i] + _dot_nt(q0[i], st[h])
            st[h] = st[h] * dec[i] + kv[i]
            o_c.append(o * lax.rsqrt(jnp.mean(o * o, axis=-1, keepdims=True) + EPS))
        o_rows.append(jnp.concatenate(o_c, axis=1))
    for h in range(HG_HEADS):
        st_ref[h] = st[h]
    return jnp.concatenate(o_rows, axis=0) * gate


def _rwkv7_block(p_ref, mix_ref, w0_ref, a0_ref, lora_ref, g2_ref, kk_ref, ka_ref, rk_ref,
                 lnw_ref, lnb_ref, tril, seg, s_ref, prev_ref):
    rows = p_ref.shape[0]
    nchunk = rows // CHUNK
    p = p_ref[...]
    rowi = lax.broadcasted_iota(jnp.int32, (rows, RW_COLS), 0)
    shifted = jnp.where(rowi == 0, prev_ref[...], pltpu.roll(p, 1, axis=0))
    prev_ref[...] = p[rows - 1:rows]
    pf = p + (shifted - p) * mix_ref[...]

    r = pf[:, 0:RW_WIDTH]
    k = pf[:, RW_WIDTH:2 * RW_WIDTH]
    v = pf[:, 2 * RW_WIDTH:3 * RW_WIDTH]
    d_wa = pf[:, 3 * RW_WIDTH:3 * RW_WIDTH + LANES]
    d_g = pf[:, 3 * RW_WIDTH + LANES:3 * RW_WIDTH + 2 * LANES]
    lane_r = lax.broadcasted_iota(jnp.int32, (rows, LANES), 1)
    lora = _dot(jnp.where(lane_r < RW_HEAD, jnp.tanh(d_wa), d_wa), lora_ref[...])
    u = w0_ref[...] + lora[:, 0:RW_WIDTH]
    ld = -math.exp(-0.5) * _sigmoid(u)
    alpha = _sigmoid(a0_ref[...] + lora[:, RW_WIDTH:2 * RW_WIDTH])
    gate = _dot(_sigmoid(d_g), g2_ref[...])

    kk = k * kk_ref[...]
    n2 = _head_sums(kk * kk, seg)
    kk = kk * jnp.minimum(lax.rsqrt(n2), 1e12)
    km = k * (1.0 + (alpha - 1.0) * ka_ref[...])
    cum = _cumsum_rows(tril, ld)
    cum_prev = cum - ld
    kka = kk * alpha
    yield

    lane =lax.broadcasted_iota(jnp.int32, (CHUNK, LANES), 1)
    m0 = lane < RW_HEAD
    t_i = lax.broadcasted_iota(jnp.int32, (CHUNK, LANES), 0)
    s_i = lane & (RW_HEAD - 1)
    strict = s_i < t_i
    incl = s_i <= t_i
    eye = (s_i == t_i).astype(F32)
    levels = [((t_i >> (j + 1)) == (s_i >> (j + 1))) & ((t_i >> j) != (s_i >> j)) & strict
              for j in range(6)]
    r128 = lax.broadcasted_iota(jnp.int32, (LANES, LANES), 0)
    c128 = lax.broadcasted_iota(jnp.int32, (LANES, LANES), 1)
    same_head = (r128 >= RW_HEAD) == (c128 >= RW_HEAD)
    diag128 = r128 == c128
    zeros = jnp.zeros((CHUNK, LANES), F32)
    zeros128 = jnp.zeros((LANES, LANES), F32)

    def bd(x):
        return jnp.concatenate([jnp.where(m0, x, 0.0), jnp.where(m0, 0.0, x)], axis=0)

    ac, bc, kc, rc, a0, r0, bh, kh, vv, gam = ([] for _ in range(10))
    for c in range(nchunk):
        rs = slice(c * CHUNK, (c + 1) * CHUNK)
        cum_c = cum[rs]
        c_mid = cum_c[CHUNK // 2 - 1:CHUNK // 2]
        c_last = cum_c[CHUNK - 1:CHUNK]
        e_mid = jnp.exp(c_mid)
        e_tail = jnp.exp(c_last - c_mid)
        gamma = jnp.exp(c_last)
        dn = jnp.exp(c_mid - cum_c)
        ac_c = -kk[rs] * jnp.exp(cum_prev[rs] - c_mid)
        bc_c = kka[rs] * dn
        kc_c = km[rs] * dn
        rc_c = r[rs] * jnp.exp(cum_c - c_mid)
        a0_c = ac_c * e_mid
        r0_c = rc_c * e_mid
        bh_c = bc_c * e_tail
        kh_c = kc_c * e_tail
        v_c = v[rs]
        for pr in range(PAIRS):
            sl = slice(pr * LANES, (pr + 1) * LANES)
            for lst, val in ((ac, ac_c), (bc, bc_c), (kc, kc_c), (rc, rc_c), (a0, a0_c), (r0, r0_c),
                             (bh, bh_c), (kh, kh_c), (vv, v_c), (gam, gamma)):
                lst.append(val[:, sl])
    chains = range(nchunk * PAIRS)
    yield

    gram =[_dot_nt(jnp.concatenate([ac[i], rc[i]], axis=0),
                    jnp.concatenate([bd(bc[i]), bd(kc[i])], axis=0)) for i in chains]
    n_ab = [jnp.where(strict, gram[i][0:CHUNK, 0:LANES], 0.0) for i in chains]
    m_ak = [jnp.where(strict, gram[i][0:CHUNK, LANES:2 * LANES], 0.0) for i in chains]
    p_rb = [jnp.where(incl, gram[i][CHUNK:2 * CHUNK, 0:LANES], 0.0) for i in chains]
    p_rk = [jnp.where(incl, gram[i][CHUNK:2 * CHUNK, LANES:2 * LANES], 0.0) for i in chains]
    x1 = [_dot(m_ak[i], bd(vv[i])) for i in chains]
    tinv = [eye + jnp.where(levels[0], n_ab[i], 0.0) for i in chains]
    for j in range(1, 6):
        fj = [_dot(jnp.where(levels[j], n_ab[i], 0.0), bd(tinv[i])) for i in chains]
        tinv = [tinv[i] + _dot(tinv[i], bd(fj[i])) for i in chains]
    uw = [_dot(tinv[i], jnp.concatenate([bd(x1[i]), bd(a0[i])], axis=1)) for i in chains]
    u_loc = [uw[i][:, 0:LANES] for i in chains]
    w_eff = [uw[i][:, LANES:2 * LANES] for i in chains]
    yq = [_dot(jnp.concatenate([p_rb[i], p_rk[i]], axis=1),
               jnp.concatenate([jnp.concatenate([bd(u_loc[i]), bd(w_eff[i])], axis=1),
                                jnp.concatenate([bd(vv[i]), zeros128], axis=1)], axis=0))
          for i in chains]
    gd = [_dot_tn(jnp.concatenate([jnp.concatenate([u_loc[i], w_eff[i]], axis=1),
                                   jnp.concatenate([vv[i], zeros], axis=1)], axis=0),
                  jnp.concatenate([bh[i], kh[i]], axis=0)) for i in chains]
    d_t = [jnp.where(m0, gd[i][0:CHUNK], gd[i][CHUNK:LANES]) for i in chains]
    g_t = [jnp.where(same_head, gd[i][LANES:2 * LANES], 0.0)
           + jnp.where(diag128, jnp.broadcast_to(gam[i], (LANES, LANES)), 0.0) for i in chains]
    q_eff = [r0[i] + yq[i][:, LANES:2 * LANES] for i in chains]

    st = [s_ref[pr] for pr in range(PAIRS)]
    y_rows = []
    for c in range(nchunk):
        st_in = st
        st = [_dot(st_in[pr], g_t[c * PAIRS + pr]) + d_t[c * PAIRS + pr] for pr in range(PAIRS)]
        y_rows.append(jnp.concatenate(
            [_dot_nt(q_eff[c * PAIRS + pr], bd(st_in[pr])) + yq[c * PAIRS + pr][:, 0:LANES]
             for pr in range(PAIRS)], axis=1))
    for pr in range(PAIRS):
        s_ref[pr] = st[pr]
    y = jnp.concatenate(y_rows, axis=0)

    inv_n = 1.0 / RW_HEAD
    mu = _head_sums(y, seg) * inv_n
    yc = y - mu
    var = _head_sums(yc * yc, seg) * inv_n
    yn = yc * lax.rsqrt(var + RW_GN_EPS) * lnw_ref[...] + lnb_ref[...]
    bonus = _head_sums(r * km * rk_ref[...], seg) * v
    return (yn + bonus) * gate


def _mixer_kernel(x_ref, n1w_ref, win_ref, lbp_ref, hgnw_ref, mix_ref, w0_ref, a0_ref, lora_ref,
                  g2_ref, kk_ref, ka_ref, rk_ref, lnw_ref, lnb_ref, tril_ref, seg_ref, o_ref,
                  proj_a, proj_b, hst_ref, s_ref, prev_ref):
    c = pl.program_id(1)

    @pl.when(c == 0)
    def _():
        proj_b[...] = jnp.zeros_like(proj_b)
        hst_ref[...] = jnp.zeros_like(hst_ref)
        s_ref[...] = jnp.zeros_like(s_ref)
        prev_ref[...] = jnp.zeros_like(prev_ref)

    def finish(gen):
        try:
            next(gen)
        except StopIteration as done:
            return done.value
        raise AssertionError("mixer generator has more phases than the schedule below")

    def step(p_out, p_in):
        h = _rmsnorm_bf16(x_ref[...], n1w_ref[...])
        width = IN_COLS // 5

        def project(j):
            cols = slice(j * width, (j + 1) * width)
            p_out[:, cols] = jnp.dot(h, win_ref[:, cols], preferred_element_type=F32)

        tril = tril_ref[...]
        hg = _hgrn2_block(p_in.at[:, 0:HG_COLS], lbp_ref, hgnw_ref, tril, hst_ref)
        rw = _rwkv7_block(p_in.at[:, HG_COLS:IN_COLS], mix_ref, w0_ref, a0_ref, lora_ref, g2_ref,
                          kk_ref, ka_ref, rk_ref, lnw_ref, lnb_ref, tril, seg_ref[...], s_ref,
                          prev_ref)
        project(0)
        next(hg)
        project(1)
        next(rw)
        project(2)
        next(rw)
        project(3)
        o_ref[:, 0:HG_WIDTH] = finish(hg).astype(o_ref.dtype)
        project(4)
        o_ref[:, HG_WIDTH:D_MODEL] = finish(rw).astype(o_ref.dtype)

    @pl.when(c % 2 == 0)
    def _():
        step(proj_a, proj_b)

    @pl.when(c % 2 == 1)
    def _():
        step(proj_b, proj_a)


def _mixers(x, norm1_w, w_in, lbp, hg_norm_w, mix, w0, a0, lora_w, g2, k_k, k_a, r_k, ln_w, ln_b,
            tril, seg):
    batch, seq, _ = x.shape
    rows = tril.shape[0]
    nblk = seq // rows
    const = lambda shape: pl.BlockSpec(shape, lambda b, c: (0, 0), pipeline_mode=pl.Buffered(1))
    vec = lambda n: const((1, n))
    return pl.pallas_call(
        _mixer_kernel,
        grid=(batch, nblk + 1),
        in_specs=[
            pl.BlockSpec((None, rows, D_MODEL), lambda b, c: (b, jnp.minimum(c, nblk - 1), 0)),
            vec(D_MODEL), const((D_MODEL, IN_COLS)),
            const((2, HG_WIDTH)), vec(HG_WIDTH),
            vec(RW_COLS), vec(RW_WIDTH), vec(RW_WIDTH),
            const((LANES, 2 * RW_WIDTH)), const((LANES, RW_WIDTH)),
            vec(RW_WIDTH), vec(RW_WIDTH), vec(RW_WIDTH), vec(RW_WIDTH), vec(RW_WIDTH),
            const((rows, rows)), const((MXU_WIDTH, MXU_WIDTH)),
        ],
        out_specs=pl.BlockSpec((None, rows, D_MODEL), lambda b, c: (b, jnp.maximum(c - 1, 0), 0)),
        out_shape=jax.ShapeDtypeStruct((batch, seq, D_MODEL), BF16),
        scratch_shapes=[pltpu.VMEM((rows, IN_COLS), F32), pltpu.VMEM((rows, IN_COLS), F32),
                        pltpu.VMEM((HG_HEADS, HG_DK, HG_DK), F32),
                        pltpu.VMEM((PAIRS, RW_HEAD, LANES), F32),
                        pltpu.VMEM((1, RW_COLS), F32)],
        compiler_params=pltpu.CompilerParams(
            dimension_semantics=("parallel", "arbitrary"), vmem_limit_bytes=VMEM_LIMIT),
        name="mixers",
    )(x, norm1_w, w_in, lbp, hg_norm_w, mix, w0, a0, lora_w, g2, k_k, k_a, r_k, ln_w, ln_b,
      tril, seg)


def _ffn_kernel(x_ref, o_ref, wo_ref, n2_ref, w1_ref, w3_ref, w2_ref, fn_ref, out_ref):
    x1 = x_ref[...] + jnp.dot(o_ref[...], wo_ref[...], preferred_element_type=F32)
    h = _rmsnorm_bf16(x1, n2_ref[...])
    a = jnp.dot(h, w1_ref[...], preferred_element_type=F32)
    b = jnp.dot(h, w3_ref[...], preferred_element_type=F32)
    hm = (_silu(a) * b).astype(BF16)
    x2 = x1 + jnp.dot(hm, w2_ref[...], preferred_element_type=F32)
    ms2 = jnp.mean(x2 * x2, axis=-1, keepdims=True)
    out_ref[...] = x2 * lax.rsqrt(ms2 + EPS) * fn_ref[...]


def _ffn(x2, o_mix, w_out, norm2_w, w1, w3, w2, final_w, tm):
    rows = x2.shape[0]
    const = lambda r, c_: pl.BlockSpec((r, c_), lambda i: (0, 0), pipeline_mode=pl.Buffered(1))
    return pl.pallas_call(
        _ffn_kernel,
        grid=(rows // tm,),
        in_specs=[
            pl.BlockSpec((tm, D_MODEL), lambda i: (i, 0)),
            pl.BlockSpec((tm, D_MODEL), lambda i: (i, 0)),
            const(D_MODEL, D_MODEL), const(1, D_MODEL),
            const(D_MODEL, D_FF), const(D_MODEL, D_FF), const(D_FF, D_MODEL),
            const(1, D_MODEL),
        ],
        out_specs=pl.BlockSpec((tm, D_MODEL), lambda i: (i, 0)),
        out_shape=jax.ShapeDtypeStruct((rows, D_MODEL), F32),
        compiler_params=pltpu.CompilerParams(
            dimension_semantics=("parallel",), vmem_limit_bytes=VMEM_LIMIT),
        name="outproj_ffn",
    )(x2, o_mix, w_out, norm2_w, w1, w3, w2, final_w)


def kernel(x, norm1_w, w_in, hg_lower_bounds, hg_norm_w, rw_mix, rw_w0, rw_w2, rw_a0, rw_a2,
           rw_g2, rw_k_k, rw_k_a, rw_r_k, rw_ln_w, rw_ln_b, w_out, norm2_w, ffn_w1, ffn_w3,
           ffn_w2, final_norm_w):
    batch, seq, _ = x.shape
    rows = batch * seq
    row = lambda z: z.reshape(1, -1)

    tril = jnp.kron(jnp.eye(MIX_BLOCK_CHUNKS, dtype=BF16), jnp.tril(jnp.ones((CHUNK, CHUNK), BF16)))
    zero = jnp.zeros((64, RW_WIDTH), F32)
    lora_w = jnp.concatenate([jnp.concatenate([rw_w2[0], zero], axis=1),
                              jnp.concatenate([zero, rw_a2[0]], axis=1)], axis=0).astype(BF16)
    head_id = jnp.arange(MXU_WIDTH) // RW_HEAD
    seg = (head_id[:, None] == head_id[None, :]).astype(BF16)
    o_mix = _mixers(x, row(norm1_w[0]), w_in[0].astype(BF16), hg_lower_bounds, row(hg_norm_w[0]),
                    row(rw_mix[0]), row(rw_w0[0]), row(rw_a0[0]), lora_w, rw_g2[0].astype(BF16),
                    row(rw_k_k[0]), row(rw_k_a[0]), row(rw_r_k[0]), row(rw_ln_w[0]),
                    row(rw_ln_b[0]), tril, seg)

    out = _ffn(x.reshape(rows, D_MODEL), o_mix.reshape(rows, D_MODEL), w_out[0].astype(BF16),
               row(norm2_w[0]), ffn_w1[0].astype(BF16), ffn_w3[0].astype(BF16),
               ffn_w2[0].astype(BF16), row(final_norm_w), tm=512)
    return out.reshape(batch, seq, D_MODEL)
```

```python
import math

import jax
import jax.numpy as jnp
from jax import lax
from jax.experimental import pallas as pl
from jax.experimental.pallas import tpu as pltpu

F32 = jnp.float32
BF16 = jnp.bfloat16

D_MODEL = 1024
CHUNK = 64
EPS = 1e-6
HG_WIDTH = 512
HG_DK = 128
HG_HEADS = 4
RW_WIDTH = 512
RW_HEAD = 64
RW_HEADS = 8
RW_GN_EPS = 64e-5
HG_COLS = 4 * HG_WIDTH
RW_COLS = 3 * RW_WIDTH + 64 + 64 + 128
IN_COLS = HG_COLS + RW_COLS
D_FF = 2816
LANES = 128
MXU_WIDTH = 256
PAIRS = RW_WIDTH // LANES
MIX_BLOCK_CHUNKS = 4
PROLOGUE_TILES = 20
PACE = 8
VMEM_LIMIT = 56 * 1024 * 1024


def _dot(a, b):
    return jnp.dot(a.astype(BF16), b.astype(BF16), preferred_element_type=F32)


def _dot_nt(a, b):
    return lax.dot_general(a.astype(BF16), b.astype(BF16), (((1,), (1,)), ((), ())),
                           preferred_element_type=F32)


def _dot_tn(a, b):
    return jnp.dot(a.T.astype(BF16), b.astype(BF16), preferred_element_type=F32)


def _cumsum_rows(tril, x):
    hi = x.astype(BF16)
    lo = (x - hi.astype(F32)).astype(BF16)
    return (jnp.dot(tril, hi, preferred_element_type=F32)
            + jnp.dot(tril, lo, preferred_element_type=F32))


def _head_sums(x, seg):
    w = seg.shape[0]
    return jnp.concatenate([_dot(x[:, n:n + w], seg) for n in range(0, x.shape[1], w)], axis=1)


def _sigmoid(x):
    return 1.0 / (1.0 + jnp.exp(-x))


def _silu(x):
    return x * _sigmoid(x)


def _rmsnorm_bf16(x, w):
    ms = jnp.mean(x * x, axis=-1, keepdims=True)
    return (x * lax.rsqrt(ms + EPS) * w).astype(BF16)


def _hgrn2_block(p_ref, lbp_ref, nw_ref, tril, st_ref, paced):
    nchunk = p_ref.shape[0] // CHUNK
    lbp = lbp_ref[...]
    m = jnp.maximum(lbp[0:1], lbp[1:2])
    e0 = jnp.exp(lbp[0:1] - m)
    e1 = jnp.exp(lbp[1:2] - m)
    lb = e0 / (e0 + e1)

    q = p_ref[:, 0:HG_WIDTH]
    f = p_ref[:, HG_WIDTH:2 * HG_WIDTH]
    v = p_ref[:, 2 * HG_WIDTH:3 * HG_WIDTH]
    g = p_ref[:, 3 * HG_WIDTH:4 * HG_WIDTH]
    qf = _silu(q)
    fs = lb + (1.0 - lb) * _sigmoid(f)
    kf = 1.0 - fs
    b = _cumsum_rows(tril, jnp.log(fs))
    gate = _silu(g) * nw_ref[...]

    row = lax.broadcasted_iota(jnp.int32, (CHUNK, CHUNK), 0)
    col = lax.broadcasted_iota(jnp.int32, (CHUNK, CHUNK), 1)
    causal = col <= row

    qs, ks, q0, kh, vv, dec = ([] for _ in range(6))
    for c in range(nchunk):
        rs = slice(c * CHUNK, (c + 1) * CHUNK)
        b_c = b[rs]
        b_mid = b_c[CHUNK // 2:CHUNK // 2 + 1]
        b_last = b_c[CHUNK - 1:CHUNK]
        qs_c = qf[rs] * jnp.exp(b_c - b_mid)
        ks_c = kf[rs] * jnp.exp(b_mid - b_c)
        q0_c = qs_c * jnp.exp(b_mid)
        kh_c = ks_c * jnp.exp(b_last - b_mid)
        dec_c = jnp.exp(b_last)
        v_c = v[rs]
        for h in range(HG_HEADS):
            sl = slice(h * HG_DK, (h + 1) * HG_DK)
            for lst, val in ((qs, qs_c), (ks, ks_c), (q0, q0_c), (kh, kh_c), (vv, v_c),
                             (dec, dec_c)):
                lst.append(val[:, sl])
    items = range(nchunk * HG_HEADS)
    yield
    scores = [jnp.where(causal, paced(_dot_nt(qs[i], ks[i])), 0.0) for i in items]
    kv = [paced(_dot_tn(vv[i], kh[i])) for i in items]
    o_in = [paced(_dot(scores[i], vv[i])) for i in items]
    yield
    st = [st_ref[h] for h in range(HG_HEADS)]
    o_rows = []
    for c in range(nchunk):
        o_c = []
        for h in range(HG_HEADS):
            i = c * HG_HEADS + h
            o = o_in[i] + paced(_dot_nt(q0[i], st[h]))
            st[h] = st[h] * dec[i] + kv[i]
            o_c.append(o * lax.rsqrt(jnp.mean(o * o, axis=-1, keepdims=True) + EPS))
        o_rows.append(jnp.concatenate(o_c, axis=1))
    for h in range(HG_HEADS):
        st_ref[h] = st[h]
    return jnp.concatenate(o_rows, axis=0) * gate


def _rwkv7_block(p_ref, mix_ref, w0_ref, a0_ref, lora_ref, g2_ref, kk_ref, ka_ref, rk_ref,
                 lnw_ref, lnb_ref, tril, seg, s_ref, prev_ref, paced):
    rows = p_ref.shape[0]
    nchunk = rows // CHUNK
    p = p_ref[...]
    rowi = lax.broadcasted_iota(jnp.int32, (rows, RW_COLS), 0)
    shifted = jnp.where(rowi == 0, prev_ref[...], pltpu.roll(p, 1, axis=0))
    prev_ref[...] = p[rows - 1:rows]
    pf = p + (shifted - p) * mix_ref[...]

    r = pf[:, 0:RW_WIDTH]
    k = pf[:, RW_WIDTH:2 * RW_WIDTH]
    v = pf[:, 2 * RW_WIDTH:3 * RW_WIDTH]
    d_wa = pf[:, 3 * RW_WIDTH:3 * RW_WIDTH + LANES]
    d_g = pf[:, 3 * RW_WIDTH + LANES:3 * RW_WIDTH + 2 * LANES]
    lane_r = lax.broadcasted_iota(jnp.int32, (rows, LANES), 1)
    lora = _dot(jnp.where(lane_r < RW_HEAD, jnp.tanh(d_wa), d_wa), lora_ref[...])
    u = w0_ref[...] + lora[:, 0:RW_WIDTH]
    ld = -math.exp(-0.5) * _sigmoid(u)
    alpha = _sigmoid(a0_ref[...] + lora[:, RW_WIDTH:2 * RW_WIDTH])
    gate = _dot(_sigmoid(d_g), g2_ref[...])

    kk = k * kk_ref[...]
    n2 = _head_sums(kk * kk, seg)
    kk = kk * jnp.minimum(lax.rsqrt(n2), 1e12)
    km = k * (1.0 + (alpha - 1.0) * ka_ref[...])
    cum = _cumsum_rows(tril, ld)
    cum_prev = cum - ld
    kka = kk * alpha
    yield

    lane =lax.broadcasted_iota(jnp.int32, (CHUNK, LANES), 1)
    m0 = lane < RW_HEAD
    t_i = lax.broadcasted_iota(jnp.int32, (CHUNK, LANES), 0)
    s_i = lane & (RW_HEAD - 1)
    strict = s_i < t_i
    incl = s_i <= t_i
    eye = (s_i == t_i).astype(F32)
    levels = [((t_i >> (j + 1)) == (s_i >> (j + 1))) & ((t_i >> j) != (s_i >> j)) & strict
              for j in range(6)]
    r128 = lax.broadcasted_iota(jnp.int32, (LANES, LANES), 0)
    c128 = lax.broadcasted_iota(jnp.int32, (LANES, LANES), 1)
    same_head = (r128 >= RW_HEAD) == (c128 >= RW_HEAD)
    diag128 = r128 == c128
    zeros = jnp.zeros((CHUNK, LANES), F32)

    def bd(x):
        return jnp.concatenate([jnp.where(m0, x, 0.0), jnp.where(m0, 0.0, x)], axis=0)

    ac, bc, kc, rc, a0, r0, bh, kh, vv, gam = ([] for _ in range(10))
    for c in range(nchunk):
        rs = slice(c * CHUNK, (c + 1) * CHUNK)
        cum_c = cum[rs]
        c_mid = cum_c[CHUNK // 2 - 1:CHUNK // 2]
        c_last = cum_c[CHUNK - 1:CHUNK]
        e_mid = jnp.exp(c_mid)
        e_tail = jnp.exp(c_last - c_mid)
        gamma = jnp.exp(c_last)
        dn = jnp.exp(c_mid - cum_c)
        ac_c = -kk[rs] * jnp.exp(cum_prev[rs] - c_mid)
        bc_c = kka[rs] * dn
        kc_c = km[rs] * dn
        rc_c = r[rs] * jnp.exp(cum_c - c_mid)
        a0_c = ac_c * e_mid
        r0_c = rc_c * e_mid
        bh_c = bc_c * e_tail
        kh_c = kc_c * e_tail
        v_c = v[rs]
        for pr in range(PAIRS):
            sl = slice(pr * LANES, (pr + 1) * LANES)
            for lst, val in ((ac, ac_c), (bc, bc_c), (kc, kc_c), (rc, rc_c), (a0, a0_c), (r0, r0_c),
                             (bh, bh_c), (kh, kh_c), (vv, v_c), (gam, gamma)):
                lst.append(val[:, sl])
    chains = range(nchunk * PAIRS)
    yield

    gram = [paced(_dot_nt(jnp.concatenate([ac[i], rc[i]], axis=0),
                          jnp.concatenate([bd(bc[i]), bd(kc[i])], axis=0)))
            for i in chains]
    n_ab = [jnp.where(strict, gram[i][0:CHUNK, 0:LANES], 0.0) for i in chains]
    m_ak = [jnp.where(strict, gram[i][0:CHUNK, LANES:2 * LANES], 0.0) for i in chains]
    p_rb = [jnp.where(incl, gram[i][CHUNK:2 * CHUNK, 0:LANES], 0.0) for i in chains]
    p_rk = [jnp.where(incl, gram[i][CHUNK:2 * CHUNK, LANES:2 * LANES], 0.0) for i in chains]
    xv = [paced(_dot(jnp.concatenate([m_ak[i], p_rk[i]], axis=0), bd(vv[i]))) for i in chains]
    x1 = [xv[i][0:CHUNK] for i in chains]
    yield
    tinv = [eye + jnp.where(levels[0], n_ab[i], 0.0) for i in chains]
    for j in range(1, 6):
        fj = [paced(_dot(jnp.where(levels[j], n_ab[i], 0.0), bd(tinv[i]))) for i in chains]
        tinv = [tinv[i] + paced(_dot(tinv[i], bd(fj[i]))) for i in chains]
        if j in (2, 4):
            yield
    uw = [paced(_dot(tinv[i], jnp.concatenate([bd(x1[i]), bd(a0[i])], axis=1)))
          for i in chains]
    yield
    u_loc =[uw[i][:, 0:LANES] for i in chains]
    w_eff = [uw[i][:, LANES:2 * LANES] for i in chains]
    yq = [paced(_dot(p_rb[i], jnp.concatenate([bd(u_loc[i]), bd(w_eff[i])], axis=1)))
          for i in chains]
    y_loc = [yq[i][:, 0:LANES] + xv[i][CHUNK:2 * CHUNK] for i in chains]
    gd = [paced(_dot_tn(jnp.concatenate([jnp.concatenate([u_loc[i], w_eff[i]], axis=1),
                                         jnp.concatenate([vv[i], zeros], axis=1)], axis=0),
                        jnp.concatenate([bh[i], kh[i]], axis=0))) for i in chains]
    d_t = [jnp.where(m0, gd[i][0:CHUNK], gd[i][CHUNK:LANES]) for i in chains]
    g_t = [jnp.where(same_head, gd[i][LANES:2 * LANES], 0.0)
           + jnp.where(diag128, jnp.broadcast_to(gam[i], (LANES, LANES)), 0.0) for i in chains]
    q_eff = [r0[i] + yq[i][:, LANES:2 * LANES] for i in chains]

    st = [s_ref[pr] for pr in range(PAIRS)]
    y_rows = []
    for c in range(nchunk):
        st_in = st
        st = [paced(_dot(st_in[pr], g_t[c * PAIRS + pr])) + d_t[c * PAIRS + pr]
              for pr in range(PAIRS)]
        y_rows.append(jnp.concatenate(
            [paced(_dot_nt(q_eff[c * PAIRS + pr], bd(st_in[pr]))) + y_loc[c * PAIRS + pr]
             for pr in range(PAIRS)], axis=1))
    for pr in range(PAIRS):
        s_ref[pr] = st[pr]
    y = jnp.concatenate(y_rows, axis=0)

    inv_n = 1.0 / RW_HEAD
    mu = _head_sums(y, seg) * inv_n
    yc = y - mu
    var = _head_sums(yc * yc, seg) * inv_n
    yn = yc * lax.rsqrt(var + RW_GN_EPS) * lnw_ref[...] + lnb_ref[...]
    bonus = _head_sums(r * km * rk_ref[...], seg) * v
    return (yn + bonus) * gate


def _mixer_kernel(x_ref, n1w_ref, win_ref, lbp_ref, hgnw_ref, mix_ref, w0_ref, a0_ref, lora_ref,
                  g2_ref, kk_ref, ka_ref, rk_ref, lnw_ref, lnb_ref, tril_ref, seg_ref, o_ref,
                  proj_a, proj_b, hst_ref, s_ref, prev_ref):
    c = pl.program_id(1)

    @pl.when(c == 0)
    def _():
        hst_ref[...] = jnp.zeros_like(hst_ref)
        s_ref[...] = jnp.zeros_like(s_ref)
        prev_ref[...] = jnp.zeros_like(prev_ref)
        proj_a[...] = jnp.dot(_rmsnorm_bf16(x_ref[...], n1w_ref[...]), win_ref[...],
                              preferred_element_type=F32)

    def finish(gen):
        try:
            next(gen)
        except StopIteration as done:
            return done.value
        raise AssertionError("mixer generator has more phases than the schedule below")

    def step(p_out, p_in):
        h = _rmsnorm_bf16(x_ref[...], n1w_ref[...])
        tiles = iter([(n, k) for n in range(0, IN_COLS, MXU_WIDTH)
                      for k in range(0, D_MODEL, MXU_WIDTH)])
        partial = [None]

        def project(count):
            for n, k in [t for _, t in zip(range(count), tiles)]:
                part = jnp.dot(h[:, k:k + MXU_WIDTH], win_ref[k:k + MXU_WIDTH, n:n + MXU_WIDTH],
                               preferred_element_type=F32)
                partial[0] = part if k == 0 else partial[0] + part
                if k + MXU_WIDTH == D_MODEL:
                    p_out[:, n:n + MXU_WIDTH] = partial[0]

        calls = [0]

        def paced(value):
            calls[0] += 1
            if calls[0] % PACE == 0:
                project(1)
            return value

        tril = tril_ref[...]
        hg = _hgrn2_block(p_in.at[:, 0:HG_COLS], lbp_ref, hgnw_ref, tril, hst_ref, paced)
        rw = _rwkv7_block(p_in.at[:, HG_COLS:IN_COLS], mix_ref, w0_ref, a0_ref, lora_ref, g2_ref,
                          kk_ref, ka_ref, rk_ref, lnw_ref, lnb_ref, tril, seg_ref[...], s_ref,
                          prev_ref, paced)
        project(PROLOGUE_TILES // 2)
        next(rw)
        project(PROLOGUE_TILES - PROLOGUE_TILES // 2)
        next(rw)
        next(rw)
        next(hg)
        next(rw)
        next(hg)
        next(rw)
        o_ref[:, 0:HG_WIDTH] = finish(hg).astype(o_ref.dtype)
        next(rw)
        o_ref[:, HG_WIDTH:D_MODEL] = finish(rw).astype(o_ref.dtype)
        project(IN_COLS * D_MODEL // MXU_WIDTH ** 2)

    @pl.when((c % 2 == 0) & (c > 0))
    def _():
        step(proj_a, proj_b)

    @pl.when(c % 2 == 1)
    def _():
        step(proj_b, proj_a)


def _mixers(x, norm1_w, w_in, lbp, hg_norm_w, mix, w0, a0, lora_w, g2, k_k, k_a, r_k, ln_w, ln_b,
            tril, seg):
    batch, seq, _ = x.shape
    rows = tril.shape[0]
    nblk = seq // rows
    const = lambda shape: pl.BlockSpec(shape, lambda b, c: (0, 0), pipeline_mode=pl.Buffered(1))
    vec = lambda n: const((1, n))
    return pl.pallas_call(
        _mixer_kernel,
        grid=(batch, nblk + 1),
        in_specs=[
            pl.BlockSpec((None, rows, D_MODEL), lambda b, c: (b, jnp.minimum(c, nblk - 1), 0)),
            vec(D_MODEL), const((D_MODEL, IN_COLS)),
            const((2, HG_WIDTH)), vec(HG_WIDTH),
            vec(RW_COLS), vec(RW_WIDTH), vec(RW_WIDTH),
            const((LANES, 2 * RW_WIDTH)), const((LANES, RW_WIDTH)),
            vec(RW_WIDTH), vec(RW_WIDTH), vec(RW_WIDTH), vec(RW_WIDTH), vec(RW_WIDTH),
            const((rows, rows)), const((MXU_WIDTH, MXU_WIDTH)),
        ],
        out_specs=pl.BlockSpec((None, rows, D_MODEL), lambda b, c: (b, jnp.maximum(c - 1, 0), 0)),
        out_shape=jax.ShapeDtypeStruct((batch, seq, D_MODEL), BF16),
        scratch_shapes=[pltpu.VMEM((rows, IN_COLS), F32), pltpu.VMEM((rows, IN_COLS), F32),
                        pltpu.VMEM((HG_HEADS, HG_DK, HG_DK), F32),
                        pltpu.VMEM((PAIRS, RW_HEAD, LANES), F32),
                        pltpu.VMEM((1, RW_COLS), F32)],
        compiler_params=pltpu.CompilerParams(
            dimension_semantics=("parallel", "arbitrary"), vmem_limit_bytes=VMEM_LIMIT),
        name="mixers",
    )(x, norm1_w, w_in, lbp, hg_norm_w, mix, w0, a0, lora_w, g2, k_k, k_a, r_k, ln_w, ln_b,
      tril, seg)


def _ffn_kernel(x_ref, o_ref, wo_ref, n2_ref, w1_ref, w3_ref, w2_ref, fn_ref, out_ref):
    x1 = x_ref[...] + jnp.dot(o_ref[...], wo_ref[...], preferred_element_type=F32)
    h = _rmsnorm_bf16(x1, n2_ref[...])
    a = jnp.dot(h, w1_ref[...], preferred_element_type=F32)
    b = jnp.dot(h, w3_ref[...], preferred_element_type=F32)
    hm = (_silu(a) * b).astype(BF16)
    x2 = x1 + jnp.dot(hm, w2_ref[...], preferred_element_type=F32)
    ms2 = jnp.mean(x2 * x2, axis=-1, keepdims=True)
    out_ref[...] = x2 * lax.rsqrt(ms2 + EPS) * fn_ref[...]


def _ffn(x2, o_mix, w_out, norm2_w, w1, w3, w2, final_w, tm):
    rows = x2.shape[0]
    const = lambda r, c_: pl.BlockSpec((r, c_), lambda i: (0, 0), pipeline_mode=pl.Buffered(1))
    return pl.pallas_call(
        _ffn_kernel,
        grid=(rows // tm,),
        in_specs=[
            pl.BlockSpec((tm, D_MODEL), lambda i: (i, 0)),
            pl.BlockSpec((tm, D_MODEL), lambda i: (i, 0)),
            const(D_MODEL, D_MODEL), const(1, D_MODEL),
            const(D_MODEL, D_FF), const(D_MODEL, D_FF), const(D_FF, D_MODEL),
            const(1, D_MODEL),
        ],
        out_specs=pl.BlockSpec((tm, D_MODEL), lambda i: (i, 0)),
        out_shape=jax.ShapeDtypeStruct((rows, D_MODEL), F32),
        compiler_params=pltpu.CompilerParams(
            dimension_semantics=("parallel",), vmem_limit_bytes=VMEM_LIMIT),
        name="outproj_ffn",
    )(x2, o_mix, w_out, norm2_w, w1, w3, w2, final_w)


def kernel(x, norm1_w, w_in, hg_lower_bounds, hg_norm_w, rw_mix, rw_w0, rw_w2, rw_a0, rw_a2,
           rw_g2, rw_k_k, rw_k_a, rw_r_k, rw_ln_w, rw_ln_b, w_out, norm2_w, ffn_w1, ffn_w3,
           ffn_w2, final_norm_w):
    batch, seq, _ = x.shape
    rows = batch * seq
    row = lambda z: z.reshape(1, -1)

    tril = jnp.kron(jnp.eye(MIX_BLOCK_CHUNKS, dtype=BF16), jnp.tril(jnp.ones((CHUNK, CHUNK), BF16)))
    zero = jnp.zeros((64, RW_WIDTH), F32)
    lora_w = jnp.concatenate([jnp.concatenate([rw_w2[0], zero], axis=1),
                              jnp.concatenate([zero, rw_a2[0]], axis=1)], axis=0).astype(BF16)
    head_id = jnp.arange(MXU_WIDTH) // RW_HEAD
    seg = (head_id[:, None] == head_id[None, :]).astype(BF16)
    o_mix = _mixers(x, row(norm1_w[0]), w_in[0].astype(BF16), hg_lower_bounds, row(hg_norm_w[0]),
                    row(rw_mix[0]), row(rw_w0[0]), row(rw_a0[0]), lora_w, rw_g2[0].astype(BF16),
                    row(rw_k_k[0]), row(rw_k_a[0]), row(rw_r_k[0]), row(rw_ln_w[0]),
                    row(rw_ln_b[0]), tril, seg)

    out = _ffn(x.reshape(rows, D_MODEL), o_mix.reshape(rows, D_MODEL), w_out[0].astype(BF16),
               row(norm2_w[0]), ffn_w1[0].astype(BF16), ffn_w3[0].astype(BF16),
               ffn_w2[0].astype(BF16), row(final_norm_w), tm=512)
    return out.reshape(batch, seq, D_MODEL)
```

```python
import math

import jax
import jax.numpy as jnp
from jax import lax
from jax.experimental import pallas as pl
from jax.experimental.pallas import tpu as pltpu

F32 = jnp.float32
BF16 = jnp.bfloat16

D_MODEL = 1024
CHUNK = 64
EPS = 1e-6
HG_WIDTH = 512
HG_DK = 128
HG_HEADS = 4
RW_WIDTH = 512
RW_HEAD = 64
RW_HEADS = 8
RW_GN_EPS = 64e-5
HG_COLS = 4 * HG_WIDTH
RW_COLS = 3 * RW_WIDTH + 64 + 64 + 128
IN_COLS = HG_COLS + RW_COLS
D_FF = 2816
LANES = 128
BF16_SUBLANES = 16
MXU_WIDTH = 256
PAIRS = RW_WIDTH // LANES
MIX_BLOCK_CHUNKS = 4
PROLOGUE_TILES = 20
PACE = 8
VMEM_LIMIT = 56 * 1024 * 1024


def _dot(a, b):
    return jnp.dot(a.astype(BF16), b.astype(BF16), preferred_element_type=F32)


def _dot_nt(a, b):
    return jnp.dot(a.astype(BF16), b.astype(F32).T.astype(BF16), preferred_element_type=F32)


def _dot_tn(a, b):
    return jnp.dot(a.T.astype(BF16), b.astype(BF16), preferred_element_type=F32)


def _cumsum_rows(tril, x):
    hi = x.astype(BF16)
    lo = (x - hi.astype(F32)).astype(BF16)
    return (jnp.dot(tril, hi, preferred_element_type=F32)
            + jnp.dot(tril, lo, preferred_element_type=F32))


def _head_sums(x, seg):
    w = seg.shape[0]
    return jnp.concatenate([_dot(x[:, n:n + w], seg) for n in range(0, x.shape[1], w)], axis=1)


def _sigmoid(x):
    return 0.5 * jnp.tanh(0.5 * x) + 0.5


def _silu(x):
    return x * _sigmoid(x)


def _rmsnorm_bf16(x, w):
    ms = jnp.mean(x * x, axis=-1, keepdims=True)
    return (x * lax.rsqrt(ms + EPS) * w).astype(BF16)


def _hgrn2_block(p_ref, lbp_ref, nw_ref, tril, st_ref, paced):
    nchunk = p_ref.shape[0] // CHUNK
    lbp = lbp_ref[...]
    m = jnp.maximum(lbp[0:1], lbp[1:2])
    e0 = jnp.exp(lbp[0:1] - m)
    e1 = jnp.exp(lbp[1:2] - m)
    lb = e0 / (e0 + e1)

    q = p_ref[:, 0:HG_WIDTH]
    f = p_ref[:, HG_WIDTH:2 * HG_WIDTH]
    v = p_ref[:, 2 * HG_WIDTH:3 * HG_WIDTH]
    g = p_ref[:, 3 * HG_WIDTH:4 * HG_WIDTH]
    qf = _silu(q)
    fs = lb + (1.0 - lb) * _sigmoid(f)
    kf = 1.0 - fs
    b = _cumsum_rows(tril, jnp.log(fs))
    gate = _silu(g) * nw_ref[...]

    row = lax.broadcasted_iota(jnp.int32, (CHUNK, CHUNK), 0)
    col = lax.broadcasted_iota(jnp.int32, (CHUNK, CHUNK), 1)
    causal = col <= row

    qs, ks, q0, kh, vv, dec = ([] for _ in range(6))
    for c in range(nchunk):
        rs = slice(c * CHUNK, (c + 1) * CHUNK)
        b_c = b[rs]
        b_mid = b_c[CHUNK // 2:CHUNK // 2 + 1]
        b_last = b_c[CHUNK - 1:CHUNK]
        qs_c = qf[rs] * jnp.exp(b_c - b_mid)
        ks_c = kf[rs] * jnp.exp(b_mid - b_c)
        q0_c = qs_c * jnp.exp(b_mid)
        kh_c = ks_c * jnp.exp(b_last - b_mid)
        dec_c = jnp.exp(b_last)
        v_c = v[rs]
        for h in range(HG_HEADS):
            sl = slice(h * HG_DK, (h + 1) * HG_DK)
            for lst, val in ((qs, qs_c), (ks, ks_c), (q0, q0_c), (kh, kh_c), (vv, v_c),
                             (dec, dec_c)):
                lst.append(val[:, sl])
    items = range(nchunk * HG_HEADS)
    yield
    scores = [jnp.where(causal, paced(_dot_nt(qs[i], ks[i])), 0.0) for i in items]
    kv = [paced(_dot_tn(vv[i], kh[i])) for i in items]
    o_in = [paced(_dot(scores[i], vv[i])) for i in items]
    yield
    st = [st_ref[h] for h in range(HG_HEADS)]
    o_rows = []
    for c in range(nchunk):
        o_c = []
        for h in range(HG_HEADS):
            i = c * HG_HEADS + h
            o = o_in[i] + paced(_dot_nt(q0[i], st[h]))
            st[h] = st[h] * dec[i] + kv[i]
            o_c.append(o * lax.rsqrt(jnp.mean(o * o, axis=-1, keepdims=True) + EPS))
        o_rows.append(jnp.concatenate(o_c, axis=1))
    for h in range(HG_HEADS):
        st_ref[h] = st[h]
    return jnp.concatenate(o_rows, axis=0) * gate


def _rwkv7_block(p_ref, mix_ref, w0_ref, a0_ref, lora_ref, g2_ref, kk_ref, ka_ref, rk_ref,
                 lnw_ref, lnb_ref, tril, seg, s_ref, prev_ref, paced):
    rows = p_ref.shape[0]
    nchunk = rows // CHUNK
    p = p_ref[...]
    rowi = lax.broadcasted_iota(jnp.int32, (rows, RW_COLS), 0)
    shifted = jnp.where(rowi == 0, prev_ref[...], pltpu.roll(p, 1, axis=0))
    prev_ref[...] = p[rows - 1:rows]
    pf = p + (shifted - p) * mix_ref[...]

    r = pf[:, 0:RW_WIDTH]
    k = pf[:, RW_WIDTH:2 * RW_WIDTH]
    v = pf[:, 2 * RW_WIDTH:3 * RW_WIDTH]
    d_wa = pf[:, 3 * RW_WIDTH:3 * RW_WIDTH + LANES]
    d_g = pf[:, 3 * RW_WIDTH + LANES:3 * RW_WIDTH + 2 * LANES]
    lane_r = lax.broadcasted_iota(jnp.int32, (rows, LANES), 1)
    lora = _dot(jnp.where(lane_r < RW_HEAD, jnp.tanh(d_wa), d_wa), lora_ref[...])
    u = w0_ref[...] + lora[:, 0:RW_WIDTH]
    ld = -math.exp(-0.5) * _sigmoid(u)
    alpha = _sigmoid(a0_ref[...] + lora[:, RW_WIDTH:2 * RW_WIDTH])
    gate = _dot(_sigmoid(d_g), g2_ref[...])

    kk = k * kk_ref[...]
    n2 = _head_sums(kk * kk, seg)
    kk = kk * jnp.minimum(lax.rsqrt(n2), 1e12)
    km = k * (1.0 + (alpha - 1.0) * ka_ref[...])
    cum = _cumsum_rows(tril, ld)
    cum_prev = cum - ld
    kka = kk * alpha
    yield

    lane = lax.broadcasted_iota(jnp.int32, (CHUNK, LANES), 1)
    m0 = lane < RW_HEAD
    t_i = lax.broadcasted_iota(jnp.int32, (CHUNK, LANES), 0)
    s_i = lane & (RW_HEAD - 1)
    strict = s_i < t_i
    incl = s_i <= t_i
    eye = (s_i == t_i).astype(F32)
    levels = [((t_i >> (j + 1)) == (s_i >> (j + 1))) & ((t_i >> j) != (s_i >> j)) & strict
              for j in range(6)]
    r128 = lax.broadcasted_iota(jnp.int32, (LANES, LANES), 0)
    c128 = lax.broadcasted_iota(jnp.int32, (LANES, LANES), 1)
    same_head = (r128 >= RW_HEAD) == (c128 >= RW_HEAD)
    diag128 = r128 == c128
    zeros = jnp.zeros((CHUNK, LANES), F32)

    def bd(x):
        return jnp.concatenate([jnp.where(m0, x, 0.0), jnp.where(m0, 0.0, x)], axis=0)

    ac, bc, kc, rc, a0, r0, bh, kh, vv, gam = ([] for _ in range(10))
    for c in range(nchunk):
        rs = slice(c * CHUNK, (c + 1) * CHUNK)
        cum_c = cum[rs]
        c_mid = cum_c[CHUNK // 2 - 1:CHUNK // 2]
        c_last = cum_c[CHUNK - 1:CHUNK]
        e_mid = jnp.exp(c_mid)
        e_tail = jnp.exp(c_last - c_mid)
        gamma = jnp.exp(c_last)
        dn = jnp.exp(c_mid - cum_c)
        ac_c = -kk[rs] * jnp.exp(cum_prev[rs] - c_mid)
        bc_c = kka[rs] * dn
        kc_c = km[rs] * dn
        rc_c = r[rs] * jnp.exp(cum_c - c_mid)
        a0_c = ac_c * e_mid
        r0_c = rc_c * e_mid
        bh_c = bc_c * e_tail
        kh_c = kc_c * e_tail
        v_c = v[rs]
        for pr in range(PAIRS):
            sl = slice(pr * LANES, (pr + 1) * LANES)
            for lst, val in ((ac, ac_c), (bc, bc_c), (kc, kc_c), (rc, rc_c), (a0, a0_c), (r0, r0_c),
                             (bh, bh_c), (kh, kh_c), (vv, v_c), (gam, gamma)):
                lst.append(val[:, sl])
    chains = range(nchunk * PAIRS)
    yield

    gram = [paced(_dot_nt(jnp.concatenate([ac[i], rc[i]], axis=0),
                          jnp.concatenate([bd(bc[i]), bd(kc[i])], axis=0)))
            for i in chains]
    n_ab = [jnp.where(strict, gram[i][0:CHUNK, 0:LANES], 0.0) for i in chains]
    m_ak = [jnp.where(strict, gram[i][0:CHUNK, LANES:2 * LANES], 0.0) for i in chains]
    p_rb = [jnp.where(incl, gram[i][CHUNK:2 * CHUNK, 0:LANES], 0.0) for i in chains]
    p_rk = [jnp.where(incl, gram[i][CHUNK:2 * CHUNK, LANES:2 * LANES], 0.0) for i in chains]
    xv = [paced(_dot(jnp.concatenate([m_ak[i], p_rk[i]], axis=0), bd(vv[i]))) for i in chains]
    x1 = [xv[i][0:CHUNK] for i in chains]
    yield
    tinv = [eye + jnp.where(levels[0], n_ab[i], 0.0) for i in chains]
    for j in range(1, 6):
        fj = [paced(_dot(jnp.where(levels[j], n_ab[i], 0.0), bd(tinv[i]))) for i in chains]
        tinv = [tinv[i] + paced(_dot(tinv[i], bd(fj[i]))) for i in chains]
        if j in (2, 4):
            yield
    uw = [paced(_dot(tinv[i], jnp.concatenate([bd(x1[i]), bd(a0[i])], axis=1)))
          for i in chains]
    yield
    u_loc = [uw[i][:, 0:LANES] for i in chains]
    w_eff = [uw[i][:, LANES:2 * LANES] for i in chains]
    yq = [paced(_dot(p_rb[i], jnp.concatenate([bd(u_loc[i]), bd(w_eff[i])], axis=1)))
          for i in chains]
    y_loc = [yq[i][:, 0:LANES] + xv[i][CHUNK:2 * CHUNK] for i in chains]
    gd = [paced(_dot_tn(jnp.concatenate([jnp.concatenate([u_loc[i], w_eff[i]], axis=1),
                                         jnp.concatenate([vv[i], zeros], axis=1)], axis=0),
                        jnp.concatenate([bh[i], kh[i]], axis=0))) for i in chains]
    d_t = [jnp.where(m0, gd[i][0:CHUNK], gd[i][CHUNK:LANES]) for i in chains]
    g_t = [jnp.where(same_head, gd[i][LANES:2 * LANES], 0.0)
           + jnp.where(diag128, jnp.broadcast_to(gam[i], (LANES, LANES)), 0.0) for i in chains]
    q_eff = [r0[i] + yq[i][:, LANES:2 * LANES] for i in chains]

    st = [s_ref[pr] for pr in range(PAIRS)]
    y_rows = []
    for c in range(nchunk):
        st_in = st
        st = [paced(_dot(st_in[pr], g_t[c * PAIRS + pr])) + d_t[c * PAIRS + pr]
              for pr in range(PAIRS)]
        y_rows.append(jnp.concatenate(
            [paced(_dot_nt(q_eff[c * PAIRS + pr], bd(st_in[pr]))) + y_loc[c * PAIRS + pr]
             for pr in range(PAIRS)], axis=1))
    for pr in range(PAIRS):
        s_ref[pr] = st[pr]
    y = jnp.concatenate(y_rows, axis=0)

    inv_n = 1.0 / RW_HEAD
    mu = _head_sums(y, seg) * inv_n
    yc = y - mu
    var = _head_sums(yc * yc, seg) * inv_n
    yn = yc * lax.rsqrt(var + RW_GN_EPS) * lnw_ref[...] + lnb_ref[...]
    bonus = _head_sums(r * km * rk_ref[...], seg) * v
    return (yn + bonus) * gate


def _mixer_kernel(x_ref, n1w_ref, win_ref, lbp_ref, hgnw_ref, mix_ref, w0_ref, a0_ref, lora_ref,
                  g2_ref, kk_ref, ka_ref, rk_ref, lnw_ref, lnb_ref, tril_ref, seg_ref,
                  wo_ref, w1_ref, w3_ref, w2_ref, o_ref, wo_bf, w1_bf, w3_bf, w2_bf,
                  proj_a, proj_b, hst_ref, s_ref, prev_ref):
    c = pl.program_id(1)
    for src, dst in ((wo_ref, wo_bf), (w1_ref, w1_bf), (w3_ref, w3_bf), (w2_ref, w2_bf)):
        dst[...] = src[...].astype(BF16)

    @pl.when(c == 0)
    def _():
        hst_ref[...] = jnp.zeros_like(hst_ref)
        s_ref[...] = jnp.zeros_like(s_ref)
        prev_ref[...] = jnp.zeros_like(prev_ref)
        proj_a[...] = jnp.dot(_rmsnorm_bf16(x_ref[...], n1w_ref[...]), win_ref[...],
                              preferred_element_type=F32)

    def finish(gen):
        try:
            next(gen)
        except StopIteration as done:
            return done.value
        raise AssertionError("mixer generator has more phases than the schedule below")

    def step(p_out, p_in):
        h = _rmsnorm_bf16(x_ref[...], n1w_ref[...])
        tiles = iter([(n, k) for n in range(0, IN_COLS, MXU_WIDTH)
                      for k in range(0, D_MODEL, MXU_WIDTH)])
        partial = [None]

        def project(count):
            for n, k in [t for _, t in zip(range(count), tiles)]:
                part = jnp.dot(h[:, k:k + MXU_WIDTH], win_ref[k:k + MXU_WIDTH, n:n + MXU_WIDTH],
                               preferred_element_type=F32)
                partial[0] = part if k == 0 else partial[0] + part
                if k + MXU_WIDTH == D_MODEL:
                    p_out[:, n:n + MXU_WIDTH] = partial[0]

        calls = [0]

        def paced(value):
            calls[0] += 1
            if calls[0] % PACE == 0:
                project(1)
            return value

        tril = tril_ref[...]
        hg = _hgrn2_block(p_in.at[:, 0:HG_COLS], lbp_ref, hgnw_ref, tril, hst_ref, paced)
        rw = _rwkv7_block(p_in.at[:, HG_COLS:IN_COLS], mix_ref, w0_ref, a0_ref, lora_ref, g2_ref,
                          kk_ref, ka_ref, rk_ref, lnw_ref, lnb_ref, tril, seg_ref[...], s_ref,
                          prev_ref, paced)
        project(PROLOGUE_TILES // 2)
        next(rw)
        project(PROLOGUE_TILES - PROLOGUE_TILES // 2)
        next(rw)
        next(rw)
        next(hg)
        next(rw)
        next(hg)
        next(rw)
        o_ref[:, 0:HG_WIDTH] = finish(hg).astype(o_ref.dtype)
        next(rw)
        o_ref[:, HG_WIDTH:D_MODEL] = finish(rw).astype(o_ref.dtype)
        project(IN_COLS * D_MODEL // MXU_WIDTH ** 2)

    @pl.when((c % 2 == 0) & (c > 0))
    def _():
        step(proj_a, proj_b)

    @pl.when(c % 2 == 1)
    def _():
        step(proj_b, proj_a)


def _cast_block_rows(rows, steps):
    for block in range(BF16_SUBLANES, rows + 1, BF16_SUBLANES):
        if rows % block == 0 and rows // block <= steps:
            return block
    raise ValueError(f"cannot convert {rows} weight rows in {steps} grid steps")


def _mixers(x, norm1_w, w_in, lbp, hg_norm_w, mix, w0, a0, lora_w, g2, k_k, k_a, r_k, ln_w, ln_b,
            tril, seg, ffn_weights):
    batch, seq, _ = x.shape
    rows = tril.shape[0]
    nblk = seq // rows
    steps = batch * (nblk + 1)
    const = lambda shape: pl.BlockSpec(shape, lambda b, c: (0, 0), pipeline_mode=pl.Buffered(1))
    vec = lambda n: const((1, n))

    def cast_spec(w):
        block = _cast_block_rows(w.shape[0], steps)
        last = w.shape[0] // block - 1
        return pl.BlockSpec((block, w.shape[1]),
                            lambda b, c: (jnp.minimum(b * (nblk + 1) + c, last), 0))

    cast_specs = [cast_spec(w) for w in ffn_weights]
    return pl.pallas_call(
        _mixer_kernel,
        grid=(batch, nblk + 1),
        in_specs=[
            pl.BlockSpec((None, rows, D_MODEL), lambda b, c: (b, jnp.minimum(c, nblk - 1), 0)),
            vec(D_MODEL), const((D_MODEL, IN_COLS)),
            const((2, HG_WIDTH)), vec(HG_WIDTH),
            vec(RW_COLS), vec(RW_WIDTH), vec(RW_WIDTH),
            const((LANES, 2 * RW_WIDTH)), const((LANES, RW_WIDTH)),
            vec(RW_WIDTH), vec(RW_WIDTH), vec(RW_WIDTH), vec(RW_WIDTH), vec(RW_WIDTH),
            const((rows, rows)), const((MXU_WIDTH, MXU_WIDTH)),
        ] + cast_specs,
        out_specs=[pl.BlockSpec((None, rows, D_MODEL),
                                lambda b, c: (b, jnp.maximum(c - 1, 0), 0))] + cast_specs,
        out_shape=[jax.ShapeDtypeStruct((batch, seq, D_MODEL), BF16)]
        + [jax.ShapeDtypeStruct(w.shape, BF16) for w in ffn_weights],
        scratch_shapes=[pltpu.VMEM((rows, IN_COLS), F32), pltpu.VMEM((rows, IN_COLS), F32),
                        pltpu.VMEM((HG_HEADS, HG_DK, HG_DK), F32),
                        pltpu.VMEM((PAIRS, RW_HEAD, LANES), F32),
                        pltpu.VMEM((1, RW_COLS), F32)],
        compiler_params=pltpu.CompilerParams(
            dimension_semantics=("parallel", "arbitrary"), vmem_limit_bytes=VMEM_LIMIT),
        name="mixers",
    )(x, norm1_w, w_in, lbp, hg_norm_w, mix, w0, a0, lora_w, g2, k_k, k_a, r_k, ln_w, ln_b,
      tril, seg, *ffn_weights)


def _ffn_kernel(x_ref, o_ref, wo_ref, n2_ref, w1_ref, w3_ref, w2_ref, fn_ref, out_ref):
    x1 = x_ref[...] + jnp.dot(o_ref[...], wo_ref[...], preferred_element_type=F32)
    h = _rmsnorm_bf16(x1, n2_ref[...])
    a = jnp.dot(h, w1_ref[...], preferred_element_type=F32)
    b = jnp.dot(h, w3_ref[...], preferred_element_type=F32)
    hm = (_silu(a) * b).astype(BF16)
    x2 = x1 + jnp.dot(hm, w2_ref[...], preferred_element_type=F32)
    ms2 = jnp.mean(x2 * x2, axis=-1, keepdims=True)
    out_ref[...] = x2 * lax.rsqrt(ms2 + EPS) * fn_ref[...]


def _ffn(x2, o_mix, w_out, norm2_w, w1, w3, w2, final_w, tm):
    rows = x2.shape[0]
    const = lambda r, c_: pl.BlockSpec((r, c_), lambda i: (0, 0), pipeline_mode=pl.Buffered(1))
    return pl.pallas_call(
        _ffn_kernel,
        grid=(rows // tm,),
        in_specs=[
            pl.BlockSpec((tm, D_MODEL), lambda i: (i, 0)),
            pl.BlockSpec((tm, D_MODEL), lambda i: (i, 0)),
            const(D_MODEL, D_MODEL), const(1, D_MODEL),
            const(D_MODEL, D_FF), const(D_MODEL, D_FF), const(D_FF, D_MODEL),
            const(1, D_MODEL),
        ],
        out_specs=pl.BlockSpec((tm, D_MODEL), lambda i: (i, 0)),
        out_shape=jax.ShapeDtypeStruct((rows, D_MODEL), F32),
        compiler_params=pltpu.CompilerParams(
            dimension_semantics=("parallel",), vmem_limit_bytes=VMEM_LIMIT),
        name="outproj_ffn",
    )(x2, o_mix, w_out, norm2_w, w1, w3, w2, final_w)


def kernel(x, norm1_w, w_in, hg_lower_bounds, hg_norm_w, rw_mix, rw_w0, rw_w2, rw_a0, rw_a2,
           rw_g2, rw_k_k, rw_k_a, rw_r_k, rw_ln_w, rw_ln_b, w_out, norm2_w, ffn_w1, ffn_w3,
           ffn_w2, final_norm_w):
    batch, seq, _ = x.shape
    rows = batch * seq
    row = lambda z: z.reshape(1, -1)

    tril = jnp.kron(jnp.eye(MIX_BLOCK_CHUNKS, dtype=BF16), jnp.tril(jnp.ones((CHUNK, CHUNK), BF16)))
    zero = jnp.zeros((64, RW_WIDTH), F32)
    lora_w = jnp.concatenate([jnp.concatenate([rw_w2[0], zero], axis=1),
                              jnp.concatenate([zero, rw_a2[0]], axis=1)], axis=0).astype(BF16)
    head_id = jnp.arange(MXU_WIDTH) // RW_HEAD
    seg = (head_id[:, None] == head_id[None, :]).astype(BF16)
    o_mix, wo_bf, w1_bf, w3_bf, w2_bf = _mixers(
        x, row(norm1_w[0]), w_in[0].astype(BF16), hg_lower_bounds, row(hg_norm_w[0]),
        row(rw_mix[0]), row(rw_w0[0]), row(rw_a0[0]), lora_w, rw_g2[0].astype(BF16),
        row(rw_k_k[0]), row(rw_k_a[0]), row(rw_r_k[0]), row(rw_ln_w[0]), row(rw_ln_b[0]), tril, seg,
        (w_out[0], ffn_w1[0], ffn_w3[0], ffn_w2[0]))

    out = _ffn(x.reshape(rows, D_MODEL), o_mix.reshape(rows, D_MODEL), wo_bf, row(norm2_w[0]),
               w1_bf, w3_bf, w2_bf, row(final_norm_w), tm=512)
    return out.reshape(batch, seq, D_MODEL)
```

```python
import math

import jax
import jax.numpy as jnp
from jax import lax
from jax.experimental import pallas as pl
from jax.experimental.pallas import tpu as pltpu

F32 = jnp.float32
BF16 = jnp.bfloat16

D_MODEL = 1024
CHUNK = 64
EPS = 1e-6
HG_WIDTH = 512
HG_DK = 128
HG_HEADS = 4
RW_WIDTH = 512
RW_HEAD = 64
RW_HEADS = 8
RW_GN_EPS = 64e-5
HG_COLS = 4 * HG_WIDTH
RW_COLS = 3 * RW_WIDTH + 64 + 64 + 128
IN_COLS = HG_COLS + RW_COLS
D_FF = 2816
LANES = 128
BF16_SUBLANES = 16
MXU_WIDTH = 256
PAIRS = RW_WIDTH // LANES
MIX_BLOCK_CHUNKS = 8
PROLOGUE_TILES = 20
PACE = 8
VMEM_LIMIT = 56 * 1024 * 1024


def _dot(a, b):
    return jnp.dot(a.astype(BF16), b.astype(BF16), preferred_element_type=F32)


def _dot_nt(a, b):
    return jnp.dot(a.astype(BF16), b.astype(F32).T.astype(BF16), preferred_element_type=F32)


def _dot_tn(a, b):
    return jnp.dot(a.T.astype(BF16), b.astype(BF16), preferred_element_type=F32)


def _cumsum_rows(tril, x):
    hi = x.astype(BF16)
    lo = (x - hi.astype(F32)).astype(BF16)
    return (jnp.dot(tril, hi, preferred_element_type=F32)
            + jnp.dot(tril, lo, preferred_element_type=F32))


def _head_sums(x, seg):
    w = seg.shape[0]
    return jnp.concatenate([_dot(x[:, n:n + w], seg) for n in range(0, x.shape[1], w)], axis=1)


def _sigmoid(x):
    return 0.5 * jnp.tanh(0.5 * x) + 0.5


def _silu(x):
    return x * _sigmoid(x)


def _rmsnorm_bf16(x, w):
    ms = jnp.mean(x * x, axis=-1, keepdims=True)
    return (x * lax.rsqrt(ms + EPS) * w).astype(BF16)


def _hgrn2_block(p_ref, lbp_ref, nw_ref, tril, st_ref, paced):
    nchunk = p_ref.shape[0] // CHUNK
    lbp = lbp_ref[...]
    m = jnp.maximum(lbp[0:1], lbp[1:2])
    e0 = jnp.exp(lbp[0:1] - m)
    e1 = jnp.exp(lbp[1:2] - m)
    lb = e0 / (e0 + e1)

    q = p_ref[:, 0:HG_WIDTH]
    f = p_ref[:, HG_WIDTH:2 * HG_WIDTH]
    v = p_ref[:, 2 * HG_WIDTH:3 * HG_WIDTH]
    g = p_ref[:, 3 * HG_WIDTH:4 * HG_WIDTH]
    qf = _silu(q)
    fs = lb + (1.0 - lb) * _sigmoid(f)
    kf = 1.0 - fs
    b = _cumsum_rows(tril, jnp.log(fs))
    gate = _silu(g) * nw_ref[...]

    row = lax.broadcasted_iota(jnp.int32, (CHUNK, CHUNK), 0)
    col = lax.broadcasted_iota(jnp.int32, (CHUNK, CHUNK), 1)
    causal = col <= row

    qs, ks, q0, kh, vv, dec = ([] for _ in range(6))
    for c in range(nchunk):
        rs = slice(c * CHUNK, (c + 1) * CHUNK)
        b_c = b[rs]
        b_mid = b_c[CHUNK // 2:CHUNK // 2 + 1]
        b_last = b_c[CHUNK - 1:CHUNK]
        qs_c = qf[rs] * jnp.exp(b_c - b_mid)
        ks_c = kf[rs] * jnp.exp(b_mid - b_c)
        q0_c = qs_c * jnp.exp(b_mid)
        kh_c = ks_c * jnp.exp(b_last - b_mid)
        dec_c = jnp.exp(b_last)
        v_c = v[rs]
        for h in range(HG_HEADS):
            sl = slice(h * HG_DK, (h + 1) * HG_DK)
            for lst, val in ((qs, qs_c), (ks, ks_c), (q0, q0_c), (kh, kh_c), (vv, v_c),
                             (dec, dec_c)):
                lst.append(val[:, sl])
    items = range(nchunk * HG_HEADS)
    yield
    scores = [jnp.where(causal, paced(_dot_nt(qs[i], ks[i])), 0.0) for i in items]
    kv = [paced(_dot_tn(vv[i], kh[i])) for i in items]
    o_in = [paced(_dot(scores[i], vv[i])) for i in items]
    yield
    st = [st_ref[h] for h in range(HG_HEADS)]
    o_rows = []
    for c in range(nchunk):
        o_c = []
        for h in range(HG_HEADS):
            i = c * HG_HEADS + h
            o = o_in[i] + paced(_dot_nt(q0[i], st[h]))
            st[h] = st[h] * dec[i] + kv[i]
            o_c.append(o * lax.rsqrt(jnp.mean(o * o, axis=-1, keepdims=True) + EPS))
        o_rows.append(jnp.concatenate(o_c, axis=1))
    for h in range(HG_HEADS):
        st_ref[h] = st[h]
    return jnp.concatenate(o_rows, axis=0) * gate


def _rwkv7_block(p_ref, mix_ref, w0_ref, a0_ref, lora_ref, g2_ref, kk_ref, ka_ref, rk_ref,
                 lnw_ref, lnb_ref, tril, seg, s_ref, prev_ref, paced):
    rows = p_ref.shape[0]
    nchunk = rows // CHUNK
    p = p_ref[...]
    rowi = lax.broadcasted_iota(jnp.int32, (rows, RW_COLS), 0)
    shifted = jnp.where(rowi == 0, prev_ref[...], pltpu.roll(p, 1, axis=0))
    prev_ref[...] = p[rows - 1:rows]
    pf = p + (shifted - p) * mix_ref[...]

    r = pf[:, 0:RW_WIDTH]
    k = pf[:, RW_WIDTH:2 * RW_WIDTH]
    v = pf[:, 2 * RW_WIDTH:3 * RW_WIDTH]
    d_wa = pf[:, 3 * RW_WIDTH:3 * RW_WIDTH + LANES]
    d_g = pf[:, 3 * RW_WIDTH + LANES:3 * RW_WIDTH + 2 * LANES]
    lane_r = lax.broadcasted_iota(jnp.int32, (rows, LANES), 1)
    lora = _dot(jnp.where(lane_r < RW_HEAD, jnp.tanh(d_wa), d_wa), lora_ref[...])
    u = w0_ref[...] + lora[:, 0:RW_WIDTH]
    ld = -math.exp(-0.5) * _sigmoid(u)
    alpha = _sigmoid(a0_ref[...] + lora[:, RW_WIDTH:2 * RW_WIDTH])
    gate = _dot(_sigmoid(d_g), g2_ref[...])

    kk = k * kk_ref[...]
    n2 = _head_sums(kk * kk, seg)
    kk = kk * jnp.minimum(lax.rsqrt(n2), 1e12)
    km = k * (1.0 + (alpha - 1.0) * ka_ref[...])
    cum = _cumsum_rows(tril, ld)
    cum_prev = cum - ld
    kka = kk * alpha
    yield

    lane = lax.broadcasted_iota(jnp.int32, (CHUNK, LANES), 1)
    m0 = lane < RW_HEAD
    t_i = lax.broadcasted_iota(jnp.int32, (CHUNK, LANES), 0)
    s_i = lane & (RW_HEAD - 1)
    strict = s_i < t_i
    incl = s_i <= t_i
    eye = (s_i == t_i).astype(F32)
    levels = [((t_i >> (j + 1)) == (s_i >> (j + 1))) & ((t_i >> j) != (s_i >> j)) & strict
              for j in range(6)]
    r128 = lax.broadcasted_iota(jnp.int32, (LANES, LANES), 0)
    c128 = lax.broadcasted_iota(jnp.int32, (LANES, LANES), 1)
    same_head = (r128 >= RW_HEAD) == (c128 >= RW_HEAD)
    diag128 = r128 == c128
    zeros = jnp.zeros((CHUNK, LANES), F32)

    def bd(x):
        return jnp.concatenate([jnp.where(m0, x, 0.0), jnp.where(m0, 0.0, x)], axis=0)

    ac, bc, kc, rc, a0, r0, bh, kh, vv, gam = ([] for _ in range(10))
    for c in range(nchunk):
        rs = slice(c * CHUNK, (c + 1) * CHUNK)
        cum_c = cum[rs]
        c_mid = cum_c[CHUNK // 2 - 1:CHUNK // 2]
        c_last = cum_c[CHUNK - 1:CHUNK]
        e_mid = jnp.exp(c_mid)
        e_tail = jnp.exp(c_last - c_mid)
        gamma = jnp.exp(c_last)
        dn = jnp.exp(c_mid - cum_c)
        ac_c = -kk[rs] * jnp.exp(cum_prev[rs] - c_mid)
        bc_c = kka[rs] * dn
        kc_c = km[rs] * dn
        rc_c = r[rs] * jnp.exp(cum_c - c_mid)
        a0_c = ac_c * e_mid
        r0_c = rc_c * e_mid
        bh_c = bc_c * e_tail
        kh_c = kc_c * e_tail
        v_c = v[rs]
        for pr in range(PAIRS):
            sl = slice(pr * LANES, (pr + 1) * LANES)
            for lst, val in ((ac, ac_c), (bc, bc_c), (kc, kc_c), (rc, rc_c), (a0, a0_c), (r0, r0_c),
                             (bh, bh_c), (kh, kh_c), (vv, v_c), (gam, gamma)):
                lst.append(val[:, sl])
    chains = range(nchunk * PAIRS)
    yield

    gram = [paced(_dot_nt(jnp.concatenate([ac[i], rc[i]], axis=0),
                          jnp.concatenate([bd(bc[i]), bd(kc[i])], axis=0)))
            for i in chains]
    n_ab = [jnp.where(strict, gram[i][0:CHUNK, 0:LANES], 0.0) for i in chains]
    m_ak = [jnp.where(strict, gram[i][0:CHUNK, LANES:2 * LANES], 0.0) for i in chains]
    p_rb = [jnp.where(incl, gram[i][CHUNK:2 * CHUNK, 0:LANES], 0.0) for i in chains]
    p_rk = [jnp.where(incl, gram[i][CHUNK:2 * CHUNK, LANES:2 * LANES], 0.0) for i in chains]
    xv = [paced(_dot(jnp.concatenate([m_ak[i], p_rk[i]], axis=0), bd(vv[i]))) for i in chains]
    x1 = [xv[i][0:CHUNK] for i in chains]
    yield
    tinv = [eye + jnp.where(levels[0], n_ab[i], 0.0) for i in chains]
    for j in range(1, 6):
        fj = [paced(_dot(jnp.where(levels[j], n_ab[i], 0.0), bd(tinv[i]))) for i in chains]
        tinv = [tinv[i] + paced(_dot(tinv[i], bd(fj[i]))) for i in chains]
        if j in (2, 4):
            yield
    uw = [paced(_dot(tinv[i], jnp.concatenate([bd(x1[i]), bd(a0[i])], axis=1)))
          for i in chains]
    yield
    u_loc = [uw[i][:, 0:LANES] for i in chains]
    w_eff = [uw[i][:, LANES:2 * LANES] for i in chains]
    yq = [paced(_dot(p_rb[i], jnp.concatenate([bd(u_loc[i]), bd(w_eff[i])], axis=1)))
          for i in chains]
    y_loc = [yq[i][:, 0:LANES] + xv[i][CHUNK:2 * CHUNK] for i in chains]
    gd = [paced(_dot_tn(jnp.concatenate([jnp.concatenate([u_loc[i], w_eff[i]], axis=1),
                                         jnp.concatenate([vv[i], zeros], axis=1)], axis=0),
                        jnp.concatenate([bh[i], kh[i]], axis=0))) for i in chains]
    d_t = [jnp.where(m0, gd[i][0:CHUNK], gd[i][CHUNK:LANES]) for i in chains]
    g_t = [jnp.where(same_head, gd[i][LANES:2 * LANES], 0.0)
           + jnp.where(diag128, jnp.broadcast_to(gam[i], (LANES, LANES)), 0.0) for i in chains]
    q_eff = [r0[i] + yq[i][:, LANES:2 * LANES] for i in chains]

    st = [s_ref[pr] for pr in range(PAIRS)]
    y_rows = []
    for c in range(nchunk):
        st_in = st
        st = [paced(_dot(st_in[pr], g_t[c * PAIRS + pr])) + d_t[c * PAIRS + pr]
              for pr in range(PAIRS)]
        y_rows.append(jnp.concatenate(
            [paced(_dot_nt(q_eff[c * PAIRS + pr], bd(st_in[pr]))) + y_loc[c * PAIRS + pr]
             for pr in range(PAIRS)], axis=1))
    for pr in range(PAIRS):
        s_ref[pr] = st[pr]
    y = jnp.concatenate(y_rows, axis=0)

    inv_n = 1.0 / RW_HEAD
    mu = _head_sums(y, seg) * inv_n
    yc = y - mu
    var = _head_sums(yc * yc, seg) * inv_n
    yn = yc * lax.rsqrt(var + RW_GN_EPS) * lnw_ref[...] + lnb_ref[...]
    bonus = _head_sums(r * km * rk_ref[...], seg) * v
    return (yn + bonus) * gate


def _mixer_kernel(x_ref, n1w_ref, win_ref, lbp_ref, hgnw_ref, mix_ref, w0_ref, a0_ref, lora_ref,
                  g2_ref, kk_ref, ka_ref, rk_ref, lnw_ref, lnb_ref, tril_ref, seg_ref,
                  wo_ref, w1_ref, w3_ref, w2_ref, o_ref, wo_bf, w1_bf, w3_bf, w2_bf,
                  proj_a, proj_b, hst_ref, s_ref, prev_ref):
    c = pl.program_id(1)
    for src, dst in ((wo_ref, wo_bf), (w1_ref, w1_bf), (w3_ref, w3_bf), (w2_ref, w2_bf)):
        dst[...] = src[...].astype(BF16)

    @pl.when(c == 0)
    def _():
        hst_ref[...] = jnp.zeros_like(hst_ref)
        s_ref[...] = jnp.zeros_like(s_ref)
        prev_ref[...] = jnp.zeros_like(prev_ref)
        proj_a[...] = jnp.dot(_rmsnorm_bf16(x_ref[...], n1w_ref[...]), win_ref[...],
                              preferred_element_type=F32)

    def finish(gen):
        try:
            next(gen)
        except StopIteration as done:
            return done.value
        raise AssertionError("mixer generator has more phases than the schedule below")

    def step(p_out, p_in):
        h = _rmsnorm_bf16(x_ref[...], n1w_ref[...])
        tiles = iter([(n, k) for n in range(0, IN_COLS, MXU_WIDTH)
                      for k in range(0, D_MODEL, MXU_WIDTH)])
        partial = [None]

        def project(count):
            for n, k in [t for _, t in zip(range(count), tiles)]:
                part = jnp.dot(h[:, k:k + MXU_WIDTH], win_ref[k:k + MXU_WIDTH, n:n + MXU_WIDTH],
                               preferred_element_type=F32)
                partial[0] = part if k == 0 else partial[0] + part
                if k + MXU_WIDTH == D_MODEL:
                    p_out[:, n:n + MXU_WIDTH] = partial[0]

        calls = [0]

        def paced(value):
            calls[0] += 1
            if calls[0] % PACE == 0:
                project(1)
            return value

        tril = tril_ref[...]
        hg = _hgrn2_block(p_in.at[:, 0:HG_COLS], lbp_ref, hgnw_ref, tril, hst_ref, paced)
        rw = _rwkv7_block(p_in.at[:, HG_COLS:IN_COLS], mix_ref, w0_ref, a0_ref, lora_ref, g2_ref,
                          kk_ref, ka_ref, rk_ref, lnw_ref, lnb_ref, tril, seg_ref[...], s_ref,
                          prev_ref, paced)
        project(PROLOGUE_TILES // 2)
        next(rw)
        project(PROLOGUE_TILES - PROLOGUE_TILES // 2)
        next(rw)
        next(rw)
        next(hg)
        next(rw)
        next(hg)
        next(rw)
        o_ref[:, 0:HG_WIDTH] = finish(hg).astype(o_ref.dtype)
        next(rw)
        o_ref[:, HG_WIDTH:D_MODEL] = finish(rw).astype(o_ref.dtype)
        project(IN_COLS * D_MODEL // MXU_WIDTH ** 2)

    @pl.when((c % 2 == 0) & (c > 0))
    def _():
        step(proj_a, proj_b)

    @pl.when(c % 2 == 1)
    def _():
        step(proj_b, proj_a)


def _cast_block_rows(rows, steps):
    for block in range(BF16_SUBLANES, rows + 1, BF16_SUBLANES):
        if rows % block == 0 and rows // block <= steps:
            return block
    raise ValueError(f"cannot convert {rows} weight rows in {steps} grid steps")


def _mixers(x, norm1_w, w_in, lbp, hg_norm_w, mix, w0, a0, lora_w, g2, k_k, k_a, r_k, ln_w, ln_b,
            tril, seg, ffn_weights):
    batch, seq, _ = x.shape
    rows = tril.shape[0]
    nblk = seq // rows
    steps = batch * (nblk + 1)
    const = lambda shape: pl.BlockSpec(shape, lambda b, c: (0, 0), pipeline_mode=pl.Buffered(1))
    vec = lambda n: const((1, n))

    def cast_spec(w):
        block = _cast_block_rows(w.shape[0], steps)
        last = w.shape[0] // block - 1
        return pl.BlockSpec((block, w.shape[1]),
                            lambda b, c: (jnp.minimum(b * (nblk + 1) + c, last), 0))

    cast_specs = [cast_spec(w) for w in ffn_weights]
    return pl.pallas_call(
        _mixer_kernel,
        grid=(batch, nblk + 1),
        in_specs=[
            pl.BlockSpec((None, rows, D_MODEL), lambda b, c: (b, jnp.minimum(c, nblk - 1), 0)),
            vec(D_MODEL), const((D_MODEL, IN_COLS)),
            const((2, HG_WIDTH)), vec(HG_WIDTH),
            vec(RW_COLS), vec(RW_WIDTH), vec(RW_WIDTH),
            const((LANES, 2 * RW_WIDTH)), const((LANES, RW_WIDTH)),
            vec(RW_WIDTH), vec(RW_WIDTH), vec(RW_WIDTH), vec(RW_WIDTH), vec(RW_WIDTH),
            const((rows, rows)), const((MXU_WIDTH, MXU_WIDTH)),
        ] + cast_specs,
        out_specs=[pl.BlockSpec((None, rows, D_MODEL),
                                lambda b, c: (b, jnp.maximum(c - 1, 0), 0))] + cast_specs,
        out_shape=[jax.ShapeDtypeStruct((batch, seq, D_MODEL), BF16)]
        + [jax.ShapeDtypeStruct(w.shape, BF16) for w in ffn_weights],
        scratch_shapes=[pltpu.VMEM((rows, IN_COLS), F32), pltpu.VMEM((rows, IN_COLS), F32),
                        pltpu.VMEM((HG_HEADS, HG_DK, HG_DK), F32),
                        pltpu.VMEM((PAIRS, RW_HEAD, LANES), F32),
                        pltpu.VMEM((1, RW_COLS), F32)],
        compiler_params=pltpu.CompilerParams(
            dimension_semantics=("parallel", "arbitrary"), vmem_limit_bytes=VMEM_LIMIT),
        name="mixers",
    )(x, norm1_w, w_in, lbp, hg_norm_w, mix, w0, a0, lora_w, g2, k_k, k_a, r_k, ln_w, ln_b,
      tril, seg, *ffn_weights)


def _ffn_kernel(x_ref, o_ref, wo_ref, n2_ref, w1_ref, w3_ref, w2_ref, fn_ref, out_ref):
    x1 = x_ref[...] + jnp.dot(o_ref[...], wo_ref[...], preferred_element_type=F32)
    h = _rmsnorm_bf16(x1, n2_ref[...])
    a = jnp.dot(h, w1_ref[...], preferred_element_type=F32)
    b = jnp.dot(h, w3_ref[...], preferred_element_type=F32)
    hm = (_silu(a) * b).astype(BF16)
    x2 = x1 + jnp.dot(hm, w2_ref[...], preferred_element_type=F32)
    ms2 = jnp.mean(x2 * x2, axis=-1, keepdims=True)
    out_ref[...] = x2 * lax.rsqrt(ms2 + EPS) * fn_ref[...]


def _ffn(x2, o_mix, w_out, norm2_w, w1, w3, w2, final_w, tm):
    rows = x2.shape[0]
    const = lambda r, c_: pl.BlockSpec((r, c_), lambda i: (0, 0), pipeline_mode=pl.Buffered(1))
    return pl.pallas_call(
        _ffn_kernel,
        grid=(rows // tm,),
        in_specs=[
            pl.BlockSpec((tm, D_MODEL), lambda i: (i, 0)),
            pl.BlockSpec((tm, D_MODEL), lambda i: (i, 0)),
            const(D_MODEL, D_MODEL), const(1, D_MODEL),
            const(D_MODEL, D_FF), const(D_MODEL, D_FF), const(D_FF, D_MODEL),
            const(1, D_MODEL),
        ],
        out_specs=pl.BlockSpec((tm, D_MODEL), lambda i: (i, 0)),
        out_shape=jax.ShapeDtypeStruct((rows, D_MODEL), F32),
        compiler_params=pltpu.CompilerParams(
            dimension_semantics=("parallel",), vmem_limit_bytes=VMEM_LIMIT),
        name="outproj_ffn",
    )(x2, o_mix, w_out, norm2_w, w1, w3, w2, final_w)


def kernel(x, norm1_w, w_in, hg_lower_bounds, hg_norm_w, rw_mix, rw_w0, rw_w2, rw_a0, rw_a2,
           rw_g2, rw_k_k, rw_k_a, rw_r_k, rw_ln_w, rw_ln_b, w_out, norm2_w, ffn_w1, ffn_w3,
           ffn_w2, final_norm_w):
    batch, seq, _ = x.shape
    rows = batch * seq
    row = lambda z: z.reshape(1, -1)

    tril = jnp.kron(jnp.eye(MIX_BLOCK_CHUNKS, dtype=BF16), jnp.tril(jnp.ones((CHUNK, CHUNK), BF16)))
    zero = jnp.zeros((64, RW_WIDTH), F32)
    lora_w = jnp.concatenate([jnp.concatenate([rw_w2[0], zero], axis=1),
                              jnp.concatenate([zero, rw_a2[0]], axis=1)], axis=0).astype(BF16)
    head_id = jnp.arange(MXU_WIDTH) // RW_HEAD
    seg = (head_id[:, None] == head_id[None, :]).astype(BF16)
    o_mix, wo_bf, w1_bf, w3_bf, w2_bf = _mixers(
        x, row(norm1_w[0]), w_in[0].astype(BF16), hg_lower_bounds, row(hg_norm_w[0]),
        row(rw_mix[0]), row(rw_w0[0]), row(rw_a0[0]), lora_w, rw_g2[0].astype(BF16),
        row(rw_k_k[0]), row(rw_k_a[0]), row(rw_r_k[0]), row(rw_ln_w[0]), row(rw_ln_b[0]), tril, seg,
        (w_out[0], ffn_w1[0], ffn_w3[0], ffn_w2[0]))

    out = _ffn(x.reshape(rows, D_MODEL), o_mix.reshape(rows, D_MODEL), wo_bf, row(norm2_w[0]),
               w1_bf, w3_bf, w2_bf, row(final_norm_w), tm=512)
    return out.reshape(batch, seq, D_MODEL)
```

```python
import functools
import math

import jax
import jax.numpy as jnp
from jax import lax
from jax.experimental import pallas as pl
from jax.experimental.pallas import tpu as pltpu

F32 = jnp.float32
BF16 = jnp.bfloat16

D_MODEL = 1024
CHUNK = 64
EPS = 1e-6
HG_WIDTH = 512
HG_DK = 128
HG_HEADS = 4
RW_WIDTH = 512
RW_HEAD = 64
RW_HEADS = 8
RW_GN_EPS = 64e-5
HG_COLS = 4 * HG_WIDTH
RW_COLS = 3 * RW_WIDTH + 64 + 64 + 128
IN_COLS = HG_COLS + RW_COLS
D_FF = 2816
LANES = 128
BF16_SUBLANES = 16
MXU_WIDTH = 256
PAIRS = RW_WIDTH // LANES
MIX_BLOCK_CHUNKS = 8
PROLOGUE_TILES = 20
PACE = 8
VMEM_LIMIT = 56 * 1024 * 1024


def _dot(a, b):
    return jnp.dot(a.astype(BF16), b.astype(BF16), preferred_element_type=F32)


def _dot_nt(a, b):
    return jnp.dot(a.astype(BF16), b.astype(F32).T.astype(BF16), preferred_element_type=F32)


def _dot_tn(a, b):
    return jnp.dot(a.T.astype(BF16), b.astype(BF16), preferred_element_type=F32)


def _cumsum_rows(tril, x):
    hi = x.astype(BF16)
    lo = (x - hi.astype(F32)).astype(BF16)
    return (jnp.dot(tril, hi, preferred_element_type=F32)
            + jnp.dot(tril, lo, preferred_element_type=F32))


def _head_sums(x, seg):
    w = seg.shape[0]
    return jnp.concatenate([_dot(x[:, n:n + w], seg) for n in range(0, x.shape[1], w)], axis=1)


def _sigmoid(x):
    return 0.5 * jnp.tanh(0.5 * x) + 0.5


def _silu(x):
    return x * _sigmoid(x)


def _rmsnorm_bf16(x, w):
    ms = jnp.mean(x * x, axis=-1, keepdims=True)
    return (x * lax.rsqrt(ms + EPS) * w).astype(BF16)


def _hgrn2_block(p_ref, lbp_ref, nw_ref, tril, st_ref, paced):
    nchunk = p_ref.shape[0] // CHUNK
    lbp = lbp_ref[...]
    m = jnp.maximum(lbp[0:1], lbp[1:2])
    e0 = jnp.exp(lbp[0:1] - m)
    e1 = jnp.exp(lbp[1:2] - m)
    lb = e0 / (e0 + e1)

    q = p_ref[:, 0:HG_WIDTH]
    f = p_ref[:, HG_WIDTH:2 * HG_WIDTH]
    v = p_ref[:, 2 * HG_WIDTH:3 * HG_WIDTH]
    g = p_ref[:, 3 * HG_WIDTH:4 * HG_WIDTH]
    qf = _silu(q)
    fs = lb + (1.0 - lb) * _sigmoid(f)
    kf = 1.0 - fs
    b = _cumsum_rows(tril, jnp.log(fs))
    gate = _silu(g) * nw_ref[...]

    row = lax.broadcasted_iota(jnp.int32, (CHUNK, CHUNK), 0)
    col = lax.broadcasted_iota(jnp.int32, (CHUNK, CHUNK), 1)
    causal = col <= row

    qs, ks, q0, kh, vv, dec = ([] for _ in range(6))
    for c in range(nchunk):
        rs = slice(c * CHUNK, (c + 1) * CHUNK)
        b_c = b[rs]
        b_mid = b_c[CHUNK // 2:CHUNK // 2 + 1]
        b_last = b_c[CHUNK - 1:CHUNK]
        qs_c = qf[rs] * jnp.exp(b_c - b_mid)
        ks_c = kf[rs] * jnp.exp(b_mid - b_c)
        q0_c = qs_c * jnp.exp(b_mid)
        kh_c = ks_c * jnp.exp(b_last - b_mid)
        dec_c = jnp.exp(b_last)
        v_c = v[rs]
        for h in range(HG_HEADS):
            sl = slice(h * HG_DK, (h + 1) * HG_DK)
            for lst, val in ((qs, qs_c), (ks, ks_c), (q0, q0_c), (kh, kh_c), (vv, v_c),
                             (dec, dec_c)):
                lst.append(val[:, sl])
    items = range(nchunk * HG_HEADS)
    yield
    scores = [jnp.where(causal, paced(_dot_nt(qs[i], ks[i])), 0.0) for i in items]
    kv = [paced(_dot_tn(vv[i], kh[i])) for i in items]
    o_in = [paced(_dot(scores[i], vv[i])) for i in items]
    yield
    st = [st_ref[h] for h in range(HG_HEADS)]
    o_rows = []
    for c in range(nchunk):
        o_c = []
        for h in range(HG_HEADS):
            i = c * HG_HEADS + h
            o = o_in[i] + paced(_dot_nt(q0[i], st[h]))
            st[h] = st[h] * dec[i] + kv[i]
            o_c.append(o * lax.rsqrt(jnp.mean(o * o, axis=-1, keepdims=True) + EPS))
        o_rows.append(jnp.concatenate(o_c, axis=1))
    for h in range(HG_HEADS):
        st_ref[h] = st[h]
    return jnp.concatenate(o_rows, axis=0) * gate


def _rwkv7_block(p_ref, mix_ref, w0_ref, a0_ref, lora_ref, g2_ref, kk_ref, ka_ref, rk_ref,
                 lnw_ref, lnb_ref, tril, seg, s_ref, prev_ref, paced):
    rows = p_ref.shape[0]
    nchunk = rows // CHUNK
    p = p_ref[...]
    rowi = lax.broadcasted_iota(jnp.int32, (rows, RW_COLS), 0)
    shifted = jnp.where(rowi == 0, prev_ref[...], pltpu.roll(p, 1, axis=0))
    prev_ref[...] = p[rows - 1:rows]
    pf = p + (shifted - p) * mix_ref[...]

    r = pf[:, 0:RW_WIDTH]
    k = pf[:, RW_WIDTH:2 * RW_WIDTH]
    v = pf[:, 2 * RW_WIDTH:3 * RW_WIDTH]
    d_wa = pf[:, 3 * RW_WIDTH:3 * RW_WIDTH + LANES]
    d_g = pf[:, 3 * RW_WIDTH + LANES:3 * RW_WIDTH + 2 * LANES]
    lane_r = lax.broadcasted_iota(jnp.int32, (rows, LANES), 1)
    lora = _dot(jnp.where(lane_r < RW_HEAD, jnp.tanh(d_wa), d_wa), lora_ref[...])
    u = w0_ref[...] + lora[:, 0:RW_WIDTH]
    ld = -math.exp(-0.5) * _sigmoid(u)
    alpha = _sigmoid(a0_ref[...] + lora[:, RW_WIDTH:2 * RW_WIDTH])
    gate = _dot(_sigmoid(d_g), g2_ref[...])

    kk = k * kk_ref[...]
    n2 = _head_sums(kk * kk, seg)
    kk = kk * jnp.minimum(lax.rsqrt(n2), 1e12)
    km = k * (1.0 + (alpha - 1.0) * ka_ref[...])
    cum = _cumsum_rows(tril, ld)
    cum_prev = cum - ld
    kka = kk * alpha
    yield

    lane = lax.broadcasted_iota(jnp.int32, (CHUNK, LANES), 1)
    m0 = lane < RW_HEAD
    t_i = lax.broadcasted_iota(jnp.int32, (CHUNK, LANES), 0)
    s_i = lane & (RW_HEAD - 1)
    strict = s_i < t_i
    incl = s_i <= t_i
    eye = (s_i == t_i).astype(F32)
    levels = [((t_i >> (j + 1)) == (s_i >> (j + 1))) & ((t_i >> j) != (s_i >> j)) & strict
              for j in range(6)]
    r128 = lax.broadcasted_iota(jnp.int32, (LANES, LANES), 0)
    c128 = lax.broadcasted_iota(jnp.int32, (LANES, LANES), 1)
    same_head = (r128 >= RW_HEAD) == (c128 >= RW_HEAD)
    diag128 = r128 == c128
    zeros = jnp.zeros((CHUNK, LANES), F32)

    def bd(x):
        return jnp.concatenate([jnp.where(m0, x, 0.0), jnp.where(m0, 0.0, x)], axis=0)

    ac, bc, kc, rc, a0, r0, bh, kh, vv, gam = ([] for _ in range(10))
    for c in range(nchunk):
        rs = slice(c * CHUNK, (c + 1) * CHUNK)
        cum_c = cum[rs]
        c_mid = cum_c[CHUNK // 2 - 1:CHUNK // 2]
        c_last = cum_c[CHUNK - 1:CHUNK]
        e_mid = jnp.exp(c_mid)
        e_tail = jnp.exp(c_last - c_mid)
        gamma = jnp.exp(c_last)
        dn = jnp.exp(c_mid - cum_c)
        ac_c = -kk[rs] * jnp.exp(cum_prev[rs] - c_mid)
        bc_c = kka[rs] * dn
        kc_c = km[rs] * dn
        rc_c = r[rs] * jnp.exp(cum_c - c_mid)
        a0_c = ac_c * e_mid
        r0_c = rc_c * e_mid
        bh_c = bc_c * e_tail
        kh_c = kc_c * e_tail
        v_c = v[rs]
        for pr in range(PAIRS):
            sl = slice(pr * LANES, (pr + 1) * LANES)
            for lst, val in ((ac, ac_c), (bc, bc_c), (kc, kc_c), (rc, rc_c), (a0, a0_c), (r0, r0_c),
                             (bh, bh_c), (kh, kh_c), (vv, v_c), (gam, gamma)):
                lst.append(val[:, sl])
    chains = range(nchunk * PAIRS)
    yield

    gram = [paced(_dot_nt(jnp.concatenate([ac[i], rc[i]], axis=0),
                          jnp.concatenate([bd(bc[i]), bd(kc[i])], axis=0)))
            for i in chains]
    n_ab = [jnp.where(strict, gram[i][0:CHUNK, 0:LANES], 0.0) for i in chains]
    m_ak = [jnp.where(strict, gram[i][0:CHUNK, LANES:2 * LANES], 0.0) for i in chains]
    p_rb = [jnp.where(incl, gram[i][CHUNK:2 * CHUNK, 0:LANES], 0.0) for i in chains]
    p_rk = [jnp.where(incl, gram[i][CHUNK:2 * CHUNK, LANES:2 * LANES], 0.0) for i in chains]
    xv = [paced(_dot(jnp.concatenate([m_ak[i], p_rk[i]], axis=0), bd(vv[i]))) for i in chains]
    x1 = [xv[i][0:CHUNK] for i in chains]
    yield
    tinv = [eye + jnp.where(levels[0], n_ab[i], 0.0) for i in chains]
    for j in range(1, 6):
        fj = [paced(_dot(jnp.where(levels[j], n_ab[i], 0.0), bd(tinv[i]))) for i in chains]
        tinv = [tinv[i] + paced(_dot(tinv[i], bd(fj[i]))) for i in chains]
        if j in (2, 4):
            yield
    uw = [paced(_dot(tinv[i], jnp.concatenate([bd(x1[i]), bd(a0[i])], axis=1)))
          for i in chains]
    yield
    u_loc = [uw[i][:, 0:LANES] for i in chains]
    w_eff = [uw[i][:, LANES:2 * LANES] for i in chains]
    yq = [paced(_dot(p_rb[i], jnp.concatenate([bd(u_loc[i]), bd(w_eff[i])], axis=1)))
          for i in chains]
    y_loc = [yq[i][:, 0:LANES] + xv[i][CHUNK:2 * CHUNK] for i in chains]
    gd = [paced(_dot_tn(jnp.concatenate([jnp.concatenate([u_loc[i], w_eff[i]], axis=1),
                                         jnp.concatenate([vv[i], zeros], axis=1)], axis=0),
                        jnp.concatenate([bh[i], kh[i]], axis=0))) for i in chains]
    d_t = [jnp.where(m0, gd[i][0:CHUNK], gd[i][CHUNK:LANES]) for i in chains]
    g_t = [jnp.where(same_head, gd[i][LANES:2 * LANES], 0.0)
           + jnp.where(diag128, jnp.broadcast_to(gam[i], (LANES, LANES)), 0.0) for i in chains]
    q_eff = [r0[i] + yq[i][:, LANES:2 * LANES] for i in chains]

    st = [s_ref[pr] for pr in range(PAIRS)]
    y_rows = []
    for c in range(nchunk):
        st_in = st
        st = [paced(_dot(st_in[pr], g_t[c * PAIRS + pr])) + d_t[c * PAIRS + pr]
              for pr in range(PAIRS)]
        y_rows.append(jnp.concatenate(
            [paced(_dot_nt(q_eff[c * PAIRS + pr], bd(st_in[pr]))) + y_loc[c * PAIRS + pr]
             for pr in range(PAIRS)], axis=1))
    for pr in range(PAIRS):
        s_ref[pr] = st[pr]
    y = jnp.concatenate(y_rows, axis=0)

    inv_n = 1.0 / RW_HEAD
    mu = _head_sums(y, seg) * inv_n
    yc = y - mu
    var = _head_sums(yc * yc, seg) * inv_n
    yn = yc * lax.rsqrt(var + RW_GN_EPS) * lnw_ref[...] + lnb_ref[...]
    bonus = _head_sums(r * km * rk_ref[...], seg) * v
    return (yn + bonus) * gate


def _mixer_kernel(nblk, x_ref, n1w_ref, win_ref, lbp_ref, hgnw_ref, mix_ref, w0_ref, a0_ref, lora_ref,
                  g2_ref, kk_ref, ka_ref, rk_ref, lnw_ref, lnb_ref, tril_ref, seg_ref,
                  wo_ref, w1_ref, w3_ref, w2_ref, o_ref, wo_bf, w1_bf, w3_bf, w2_bf,
                  proj_a, proj_b, hst_ref, s_ref, prev_ref):
    t = pl.program_id(0)
    for src, dst in ((wo_ref, wo_bf), (w1_ref, w1_bf), (w3_ref, w3_bf), (w2_ref, w2_bf)):
        dst[...] = src[...].astype(BF16)

    @pl.when(t == 0)
    def _():
        proj_a[...] = jnp.dot(_rmsnorm_bf16(x_ref[...], n1w_ref[...]), win_ref[...],
                              preferred_element_type=F32)

    @pl.when((t > 0) & ((t - 1) % nblk == 0))
    def _():
        hst_ref[...] = jnp.zeros_like(hst_ref)
        s_ref[...] = jnp.zeros_like(s_ref)
        prev_ref[...] = jnp.zeros_like(prev_ref)

    def finish(gen):
        try:
            next(gen)
        except StopIteration as done:
            return done.value
        raise AssertionError("mixer generator has more phases than the schedule below")

    def step(p_out, p_in):
        h = _rmsnorm_bf16(x_ref[...], n1w_ref[...])
        tiles = iter([(n, k) for n in range(0, IN_COLS, MXU_WIDTH)
                      for k in range(0, D_MODEL, MXU_WIDTH)])
        partial = [None]

        def project(count):
            for n, k in [t for _, t in zip(range(count), tiles)]:
                part = jnp.dot(h[:, k:k + MXU_WIDTH], win_ref[k:k + MXU_WIDTH, n:n + MXU_WIDTH],
                               preferred_element_type=F32)
                partial[0] = part if k == 0 else partial[0] + part
                if k + MXU_WIDTH == D_MODEL:
                    p_out[:, n:n + MXU_WIDTH] = partial[0]

        calls = [0]

        def paced(value):
            calls[0] += 1
            if calls[0] % PACE == 0:
                project(1)
            return value

        tril = tril_ref[...]
        hg = _hgrn2_block(p_in.at[:, 0:HG_COLS], lbp_ref, hgnw_ref, tril, hst_ref, paced)
        rw = _rwkv7_block(p_in.at[:, HG_COLS:IN_COLS], mix_ref, w0_ref, a0_ref, lora_ref, g2_ref,
                          kk_ref, ka_ref, rk_ref, lnw_ref, lnb_ref, tril, seg_ref[...], s_ref,
                          prev_ref, paced)
        project(PROLOGUE_TILES // 2)
        next(rw)
        project(PROLOGUE_TILES - PROLOGUE_TILES // 2)
        next(rw)
        next(rw)
        next(hg)
        next(rw)
        next(hg)
        next(rw)
        o_ref[:, 0:HG_WIDTH] = finish(hg).astype(o_ref.dtype)
        next(rw)
        o_ref[:, HG_WIDTH:D_MODEL] = finish(rw).astype(o_ref.dtype)
        project(IN_COLS * D_MODEL // MXU_WIDTH ** 2)

    @pl.when((t % 2 == 0) & (t > 0))
    def _():
        step(proj_a, proj_b)

    @pl.when(t % 2 == 1)
    def _():
        step(proj_b, proj_a)


def _cast_block_rows(rows, steps):
    for block in range(BF16_SUBLANES, rows + 1, BF16_SUBLANES):
        if rows % block == 0 and rows // block <= steps:
            return block
    raise ValueError(f"cannot convert {rows} weight rows in {steps} grid steps")


def _mixers(x, norm1_w, w_in, lbp, hg_norm_w, mix, w0, a0, lora_w, g2, k_k, k_a, r_k, ln_w, ln_b,
            tril, seg, ffn_weights):
    batch, seq, _ = x.shape
    rows = tril.shape[0]
    nblk = seq // rows
    total = batch * nblk
    steps = total + 1
    const = lambda shape: pl.BlockSpec(shape, lambda t: (0, 0), pipeline_mode=pl.Buffered(1))
    vec = lambda n: const((1, n))

    def block_of(i):
        return i // nblk, i % nblk, 0

    def cast_spec(w):
        block = _cast_block_rows(w.shape[0], steps)
        last = w.shape[0] // block - 1
        return pl.BlockSpec((block, w.shape[1]), lambda t: (jnp.minimum(t, last), 0))

    cast_specs = [cast_spec(w) for w in ffn_weights]
    return pl.pallas_call(
        functools.partial(_mixer_kernel, nblk),
        grid=(steps,),
        in_specs=[
            pl.BlockSpec((None, rows, D_MODEL), lambda t: block_of(jnp.minimum(t, total - 1))),
            vec(D_MODEL), const((D_MODEL, IN_COLS)),
            const((2, HG_WIDTH)), vec(HG_WIDTH),
            vec(RW_COLS), vec(RW_WIDTH), vec(RW_WIDTH),
            const((LANES, 2 * RW_WIDTH)), const((LANES, RW_WIDTH)),
            vec(RW_WIDTH), vec(RW_WIDTH), vec(RW_WIDTH), vec(RW_WIDTH), vec(RW_WIDTH),
            const((rows, rows)), const((MXU_WIDTH, MXU_WIDTH)),
        ] + cast_specs,
        out_specs=[pl.BlockSpec((None, rows, D_MODEL),
                                lambda t: block_of(jnp.maximum(t - 1, 0)))] + cast_specs,
        out_shape=[jax.ShapeDtypeStruct((batch, seq, D_MODEL), BF16)]
        + [jax.ShapeDtypeStruct(w.shape, BF16) for w in ffn_weights],
        scratch_shapes=[pltpu.VMEM((rows, IN_COLS), F32), pltpu.VMEM((rows, IN_COLS), F32),
                        pltpu.VMEM((HG_HEADS, HG_DK, HG_DK), F32),
                        pltpu.VMEM((PAIRS, RW_HEAD, LANES), F32),
                        pltpu.VMEM((1, RW_COLS), F32)],
        compiler_params=pltpu.CompilerParams(
            dimension_semantics=("arbitrary",), vmem_limit_bytes=VMEM_LIMIT),
        name="mixers",
    )(x, norm1_w, w_in, lbp, hg_norm_w, mix, w0, a0, lora_w, g2, k_k, k_a, r_k, ln_w, ln_b,
      tril, seg, *ffn_weights)


def _ffn_kernel(ntiles, x_ref, o_ref, wo_ref, n2_ref, w1_ref, w3_ref, w2_ref, fn_ref, out_ref,
                x2_scr):
    i = pl.program_id(0)

    def finalize():
        x2 = x2_scr[...]
        ms2 = jnp.mean(x2 * x2, axis=-1, keepdims=True)
        out_ref[...] = x2 * lax.rsqrt(ms2 + EPS) * fn_ref[...]

    @pl.when(i == 0)
    def _():
        x2_scr[...] = jnp.zeros_like(x2_scr)

    @pl.when(i < ntiles)
    def _():
        x1 = x_ref[...] + jnp.dot(o_ref[...], wo_ref[...], preferred_element_type=F32)
        h = _rmsnorm_bf16(x1, n2_ref[...])
        a = jnp.dot(h, w1_ref[...], preferred_element_type=F32)
        finalize()
        b = jnp.dot(h, w3_ref[...], preferred_element_type=F32)
        hm = (_silu(a) * b).astype(BF16)
        x2_scr[...] = x1 + jnp.dot(hm, w2_ref[...], preferred_element_type=F32)

    @pl.when(i == ntiles)
    def _():
        finalize()


def _ffn(x2, o_mix, w_out, norm2_w, w1, w3, w2, final_w, tm):
    rows = x2.shape[0]
    ntiles = rows // tm
    const = lambda r, c_: pl.BlockSpec((r, c_), lambda i: (0, 0), pipeline_mode=pl.Buffered(1))
    tile_in = pl.BlockSpec((tm, D_MODEL), lambda i: (jnp.minimum(i, ntiles - 1), 0))
    return pl.pallas_call(
        functools.partial(_ffn_kernel, ntiles),
        grid=(ntiles + 1,),
        in_specs=[
            tile_in, tile_in,
            const(D_MODEL, D_MODEL), const(1, D_MODEL),
            const(D_MODEL, D_FF), const(D_MODEL, D_FF), const(D_FF, D_MODEL),
            const(1, D_MODEL),
        ],
        out_specs=pl.BlockSpec((tm, D_MODEL), lambda i: (jnp.maximum(i - 1, 0), 0)),
        out_shape=jax.ShapeDtypeStruct((rows, D_MODEL), F32),
        scratch_shapes=[pltpu.VMEM((tm, D_MODEL), F32)],
        compiler_params=pltpu.CompilerParams(
            dimension_semantics=("arbitrary",), vmem_limit_bytes=VMEM_LIMIT),
        name="outproj_ffn",
    )(x2, o_mix, w_out, norm2_w, w1, w3, w2, final_w)


def kernel(x, norm1_w, w_in, hg_lower_bounds, hg_norm_w, rw_mix, rw_w0, rw_w2, rw_a0, rw_a2,
           rw_g2, rw_k_k, rw_k_a, rw_r_k, rw_ln_w, rw_ln_b, w_out, norm2_w, ffn_w1, ffn_w3,
           ffn_w2, final_norm_w):
    batch, seq, _ = x.shape
    rows = batch * seq
    row = lambda z: z.reshape(1, -1)

    tril = jnp.kron(jnp.eye(MIX_BLOCK_CHUNKS, dtype=BF16), jnp.tril(jnp.ones((CHUNK, CHUNK), BF16)))
    zero = jnp.zeros((64, RW_WIDTH), F32)
    lora_w = jnp.concatenate([jnp.concatenate([rw_w2[0], zero], axis=1),
                              jnp.concatenate([zero, rw_a2[0]], axis=1)], axis=0).astype(BF16)
    head_id = jnp.arange(MXU_WIDTH) // RW_HEAD
    seg = (head_id[:, None] == head_id[None, :]).astype(BF16)
    o_mix, wo_bf, w1_bf, w3_bf, w2_bf = _mixers(
        x, row(norm1_w[0]), w_in[0].astype(BF16), hg_lower_bounds, row(hg_norm_w[0]),
        row(rw_mix[0]), row(rw_w0[0]), row(rw_a0[0]), lora_w, rw_g2[0].astype(BF16),
        row(rw_k_k[0]), row(rw_k_a[0]), row(rw_r_k[0]), row(rw_ln_w[0]), row(rw_ln_b[0]), tril, seg,
        (w_out[0], ffn_w1[0], ffn_w3[0], ffn_w2[0]))

    out = _ffn(x.reshape(rows, D_MODEL), o_mix.reshape(rows, D_MODEL), wo_bf, row(norm2_w[0]),
               w1_bf, w3_bf, w2_bf, row(final_norm_w), tm=512)
    return out.reshape(batch, seq, D_MODEL)
```

```python
import functools
import math

import jax
import jax.numpy as jnp
from jax import lax
from jax.experimental import pallas as pl
from jax.experimental.pallas import tpu as pltpu

F32 = jnp.float32
BF16 = jnp.bfloat16

D_MODEL = 1024
CHUNK = 64
EPS = 1e-6
HG_WIDTH = 512
HG_DK = 128
HG_HEADS = 4
RW_WIDTH = 512
RW_HEAD = 64
RW_HEADS = 8
RW_GN_EPS = 64e-5
HG_COLS = 4 * HG_WIDTH
RW_COLS = 3 * RW_WIDTH + 64 + 64 + 128
IN_COLS = HG_COLS + RW_COLS
D_FF = 2816
LANES = 128
BF16_SUBLANES = 16
MXU_WIDTH = 256
PAIRS = RW_WIDTH // LANES
MIX_BLOCK_CHUNKS = 8
PROLOGUE_TILES = 20
PACE = 8
VMEM_LIMIT = 56 * 1024 * 1024


def _dot(a, b):
    return jnp.dot(a.astype(BF16), b.astype(BF16), preferred_element_type=F32)


def _dot_nt(a, b):
    return jnp.dot(a.astype(BF16), b.astype(F32).T.astype(BF16), preferred_element_type=F32)


def _dot_tn(a, b):
    return jnp.dot(a.T.astype(BF16), b.astype(BF16), preferred_element_type=F32)


def _cumsum_rows(tril, x):
    n = tril.shape[0]
    hi = x.astype(BF16)
    lo = (x - hi.astype(F32)).astype(BF16)
    return jnp.concatenate(
        [jnp.dot(tril, hi[r:r + n], preferred_element_type=F32)
         + jnp.dot(tril, lo[r:r + n], preferred_element_type=F32)
         for r in range(0, x.shape[0], n)], axis=0)


def _head_sums(x, seg):
    w = seg.shape[0]
    return jnp.concatenate([_dot(x[:, n:n + w], seg) for n in range(0, x.shape[1], w)], axis=1)


def _sigmoid(x):
    return 0.5 * jnp.tanh(0.5 * x) + 0.5


def _silu(x):
    return x * _sigmoid(x)


def _rmsnorm_bf16(x, w):
    ms = jnp.mean(x * x, axis=-1, keepdims=True)
    return (x * lax.rsqrt(ms + EPS) * w).astype(BF16)


def _hgrn2_block(p_ref, lbp_ref, nw_ref, tril, st_ref, paced):
    nchunk = p_ref.shape[0] // CHUNK
    lbp = lbp_ref[...]
    m = jnp.maximum(lbp[0:1], lbp[1:2])
    e0 = jnp.exp(lbp[0:1] - m)
    e1 = jnp.exp(lbp[1:2] - m)
    lb = e0 / (e0 + e1)

    q = p_ref[:, 0:HG_WIDTH]
    f = p_ref[:, HG_WIDTH:2 * HG_WIDTH]
    v = p_ref[:, 2 * HG_WIDTH:3 * HG_WIDTH]
    g = p_ref[:, 3 * HG_WIDTH:4 * HG_WIDTH]
    qf = _silu(q)
    fs = lb + (1.0 - lb) * _sigmoid(f)
    kf = 1.0 - fs
    b = _cumsum_rows(tril, jnp.log(fs))
    gate = _silu(g) * nw_ref[...]

    row = lax.broadcasted_iota(jnp.int32, (CHUNK, CHUNK), 0)
    col = lax.broadcasted_iota(jnp.int32, (CHUNK, CHUNK), 1)
    causal = col <= row

    qs, ks, q0, kh, vv, dec = ([] for _ in range(6))
    for c in range(nchunk):
        rs = slice(c * CHUNK, (c + 1) * CHUNK)
        b_c = b[rs]
        b_mid = b_c[CHUNK // 2:CHUNK // 2 + 1]
        b_last = b_c[CHUNK - 1:CHUNK]
        qs_c = qf[rs] * jnp.exp(b_c - b_mid)
        ks_c = kf[rs] * jnp.exp(b_mid - b_c)
        q0_c = qs_c * jnp.exp(b_mid)
        kh_c = ks_c * jnp.exp(b_last - b_mid)
        dec_c = jnp.exp(b_last)
        v_c = v[rs]
        for h in range(HG_HEADS):
            sl = slice(h * HG_DK, (h + 1) * HG_DK)
            for lst, val in ((qs, qs_c), (ks, ks_c), (q0, q0_c), (kh, kh_c), (vv, v_c),
                             (dec, dec_c)):
                lst.append(val[:, sl])
    items = range(nchunk * HG_HEADS)
    yield
    scores = [jnp.where(causal, paced(_dot_nt(qs[i], ks[i])), 0.0) for i in items]
    kv = [paced(_dot_tn(vv[i], kh[i])) for i in items]
    o_in = [paced(_dot(scores[i], vv[i])) for i in items]
    yield
    st = [st_ref[h] for h in range(HG_HEADS)]
    o_rows = []
    for c in range(nchunk):
        o_c = []
        for h in range(HG_HEADS):
            i = c * HG_HEADS + h
            o = o_in[i] + paced(_dot_nt(q0[i], st[h]))
            st[h] = st[h] * dec[i] + kv[i]
            o_c.append(o * lax.rsqrt(jnp.mean(o * o, axis=-1, keepdims=True) + EPS))
        o_rows.append(jnp.concatenate(o_c, axis=1))
    for h in range(HG_HEADS):
        st_ref[h] = st[h]
    return jnp.concatenate(o_rows, axis=0) * gate


def _rwkv7_block(p_ref, mix_ref, w0_ref, a0_ref, lora_ref, g2_ref, kk_ref, ka_ref, rk_ref,
                 lnw_ref, lnb_ref, tril, seg, s_ref, prev_ref, paced):
    rows = p_ref.shape[0]
    nchunk = rows // CHUNK
    p = p_ref[...]
    rowi = lax.broadcasted_iota(jnp.int32, (rows, RW_COLS), 0)
    shifted = jnp.where(rowi == 0, prev_ref[...], pltpu.roll(p, 1, axis=0))
    prev_ref[...] = p[rows - 1:rows]
    pf = p + (shifted - p) * mix_ref[...]

    r = pf[:, 0:RW_WIDTH]
    k = pf[:, RW_WIDTH:2 * RW_WIDTH]
    v = pf[:, 2 * RW_WIDTH:3 * RW_WIDTH]
    d_wa = pf[:, 3 * RW_WIDTH:3 * RW_WIDTH + LANES]
    d_g = pf[:, 3 * RW_WIDTH + LANES:3 * RW_WIDTH + 2 * LANES]
    lane_r = lax.broadcasted_iota(jnp.int32, (rows, LANES), 1)
    lora = _dot(jnp.where(lane_r < RW_HEAD, jnp.tanh(d_wa), d_wa), lora_ref[...])
    u = w0_ref[...] + lora[:, 0:RW_WIDTH]
    ld = -math.exp(-0.5) * _sigmoid(u)
    alpha = _sigmoid(a0_ref[...] + lora[:, RW_WIDTH:2 * RW_WIDTH])
    gate = _dot(_sigmoid(d_g), g2_ref[...])

    kk = k * kk_ref[...]
    n2 = _head_sums(kk * kk, seg)
    kk = kk * jnp.minimum(lax.rsqrt(n2), 1e12)
    km = k * (1.0 + (alpha - 1.0) * ka_ref[...])
    cum = _cumsum_rows(tril, ld)
    cum_prev = cum - ld
    kka = kk * alpha
    yield

    lane = lax.broadcasted_iota(jnp.int32, (CHUNK, LANES), 1)
    m0 = lane < RW_HEAD
    t_i = lax.broadcasted_iota(jnp.int32, (CHUNK, LANES), 0)
    s_i = lane & (RW_HEAD - 1)
    strict = s_i < t_i
    incl = s_i <= t_i
    eye = (s_i == t_i).astype(F32)
    levels = [((t_i >> (j + 1)) == (s_i >> (j + 1))) & ((t_i >> j) != (s_i >> j)) & strict
              for j in range(6)]
    r128 = lax.broadcasted_iota(jnp.int32, (LANES, LANES), 0)
    c128 = lax.broadcasted_iota(jnp.int32, (LANES, LANES), 1)
    same_head = (r128 >= RW_HEAD) == (c128 >= RW_HEAD)
    diag128 = r128 == c128
    zeros = jnp.zeros((CHUNK, LANES), F32)

    def bd(x):
        return jnp.concatenate([jnp.where(m0, x, 0.0), jnp.where(m0, 0.0, x)], axis=0)

    ac, bc, kc, rc, a0, r0, bh, kh, vv, gam = ([] for _ in range(10))
    for c in range(nchunk):
        rs = slice(c * CHUNK, (c + 1) * CHUNK)
        cum_c = cum[rs]
        c_mid = cum_c[CHUNK // 2 - 1:CHUNK // 2]
        c_last = cum_c[CHUNK - 1:CHUNK]
        e_mid = jnp.exp(c_mid)
        e_tail = jnp.exp(c_last - c_mid)
        gamma = jnp.exp(c_last)
        dn = jnp.exp(c_mid - cum_c)
        ac_c = -kk[rs] * jnp.exp(cum_prev[rs] - c_mid)
        bc_c = kka[rs] * dn
        kc_c = km[rs] * dn
        rc_c = r[rs] * jnp.exp(cum_c - c_mid)
        a0_c = ac_c * e_mid
        r0_c = rc_c * e_mid
        bh_c = bc_c * e_tail
        kh_c = kc_c * e_tail
        v_c = v[rs]
        for pr in range(PAIRS):
            sl = slice(pr * LANES, (pr + 1) * LANES)
            for lst, val in ((ac, ac_c), (bc, bc_c), (kc, kc_c), (rc, rc_c), (a0, a0_c), (r0, r0_c),
                             (bh, bh_c), (kh, kh_c), (vv, v_c), (gam, gamma)):
                lst.append(val[:, sl])
    chains = range(nchunk * PAIRS)
    yield

    gram = [paced(_dot_nt(jnp.concatenate([ac[i], rc[i]], axis=0),
                          jnp.concatenate([bd(bc[i]), bd(kc[i])], axis=0)))
            for i in chains]
    n_ab = [jnp.where(strict, gram[i][0:CHUNK, 0:LANES], 0.0) for i in chains]
    m_ak = [jnp.where(strict, gram[i][0:CHUNK, LANES:2 * LANES], 0.0) for i in chains]
    p_rb = [jnp.where(incl, gram[i][CHUNK:2 * CHUNK, 0:LANES], 0.0) for i in chains]
    p_rk = [jnp.where(incl, gram[i][CHUNK:2 * CHUNK, LANES:2 * LANES], 0.0) for i in chains]
    xv = [paced(_dot(jnp.concatenate([m_ak[i], p_rk[i]], axis=0), bd(vv[i]))) for i in chains]
    x1 = [xv[i][0:CHUNK] for i in chains]
    yield
    tinv = [eye + jnp.where(levels[0], n_ab[i], 0.0) for i in chains]
    for j in range(1, 6):
        fj = [paced(_dot(jnp.where(levels[j], n_ab[i], 0.0), bd(tinv[i]))) for i in chains]
        tinv = [tinv[i] + paced(_dot(tinv[i], bd(fj[i]))) for i in chains]
        if j in (2, 4):
            yield
    uw = [paced(_dot(tinv[i], jnp.concatenate([bd(x1[i]), bd(a0[i])], axis=1)))
          for i in chains]
    yield
    u_loc = [uw[i][:, 0:LANES] for i in chains]
    w_eff = [uw[i][:, LANES:2 * LANES] for i in chains]
    yq = [paced(_dot(p_rb[i], jnp.concatenate([bd(u_loc[i]), bd(w_eff[i])], axis=1)))
          for i in chains]
    y_loc = [yq[i][:, 0:LANES] + xv[i][CHUNK:2 * CHUNK] for i in chains]
    gd = [paced(_dot_tn(jnp.concatenate([jnp.concatenate([u_loc[i], w_eff[i]], axis=1),
                                         jnp.concatenate([vv[i], zeros], axis=1)], axis=0),
                        jnp.concatenate([bh[i], kh[i]], axis=0))) for i in chains]
    d_t = [jnp.where(m0, gd[i][0:CHUNK], gd[i][CHUNK:LANES]) for i in chains]
    g_t = [jnp.where(same_head, gd[i][LANES:2 * LANES], 0.0)
           + jnp.where(diag128, jnp.broadcast_to(gam[i], (LANES, LANES)), 0.0) for i in chains]
    q_eff = [r0[i] + yq[i][:, LANES:2 * LANES] for i in chains]

    st = [s_ref[pr] for pr in range(PAIRS)]
    y_rows = []
    for c in range(nchunk):
        st_in = st
        st = [paced(_dot(st_in[pr], g_t[c * PAIRS + pr])) + d_t[c * PAIRS + pr]
              for pr in range(PAIRS)]
        y_rows.append(jnp.concatenate(
            [paced(_dot_nt(q_eff[c * PAIRS + pr], bd(st_in[pr]))) + y_loc[c * PAIRS + pr]
             for pr in range(PAIRS)], axis=1))
    for pr in range(PAIRS):
        s_ref[pr] = st[pr]
    y = jnp.concatenate(y_rows, axis=0)

    inv_n = 1.0 / RW_HEAD
    mu = _head_sums(y, seg) * inv_n
    yc = y - mu
    var = _head_sums(yc * yc, seg) * inv_n
    yn = yc * lax.rsqrt(var + RW_GN_EPS) * lnw_ref[...] + lnb_ref[...]
    bonus = _head_sums(r * km * rk_ref[...], seg) * v
    return (yn + bonus) * gate


def _mixer_kernel(nblk, x_ref, n1w_ref, win_ref, lbp_ref, hgnw_ref, mix_ref, w0_ref, a0_ref, lora_ref,
                  g2_ref, kk_ref, ka_ref, rk_ref, lnw_ref, lnb_ref, tril_ref, seg_ref,
                  wo_ref, w1_ref, w3_ref, w2_ref, o_ref, wo_bf, w1_bf, w3_bf, w2_bf,
                  proj_a, proj_b, hst_ref, s_ref, prev_ref):
    t = pl.program_id(0)
    for src, dst in ((wo_ref, wo_bf), (w1_ref, w1_bf), (w3_ref, w3_bf), (w2_ref, w2_bf)):
        dst[...] = src[...].astype(BF16)

    @pl.when(t == 0)
    def _():
        proj_a[...] = jnp.dot(_rmsnorm_bf16(x_ref[...], n1w_ref[...]), win_ref[...],
                              preferred_element_type=F32)

    @pl.when((t > 0) & ((t - 1) % nblk == 0))
    def _():
        hst_ref[...] = jnp.zeros_like(hst_ref)
        s_ref[...] = jnp.zeros_like(s_ref)
        prev_ref[...] = jnp.zeros_like(prev_ref)

    def finish(gen):
        try:
            next(gen)
        except StopIteration as done:
            return done.value
        raise AssertionError("mixer generator has more phases than the schedule below")

    def step(p_out, p_in):
        h = _rmsnorm_bf16(x_ref[...], n1w_ref[...])
        tiles = iter([(n, k) for n in range(0, IN_COLS, MXU_WIDTH)
                      for k in range(0, D_MODEL, MXU_WIDTH)])
        partial = [None]

        def project(count):
            for n, k in [t for _, t in zip(range(count), tiles)]:
                part = jnp.dot(h[:, k:k + MXU_WIDTH], win_ref[k:k + MXU_WIDTH, n:n + MXU_WIDTH],
                               preferred_element_type=F32)
                partial[0] = part if k == 0 else partial[0] + part
                if k + MXU_WIDTH == D_MODEL:
                    p_out[:, n:n + MXU_WIDTH] = partial[0]

        calls = [0]

        def paced(value):
            calls[0] += 1
            if calls[0] % PACE == 0:
                project(1)
            return value

        tril = tril_ref[...]
        hg = _hgrn2_block(p_in.at[:, 0:HG_COLS], lbp_ref, hgnw_ref, tril, hst_ref, paced)
        rw = _rwkv7_block(p_in.at[:, HG_COLS:IN_COLS], mix_ref, w0_ref, a0_ref, lora_ref, g2_ref,
                          kk_ref, ka_ref, rk_ref, lnw_ref, lnb_ref, tril, seg_ref[...], s_ref,
                          prev_ref, paced)
        project(PROLOGUE_TILES // 2)
        next(rw)
        project(PROLOGUE_TILES - PROLOGUE_TILES // 2)
        next(rw)
        next(rw)
        next(hg)
        next(rw)
        next(hg)
        next(rw)
        o_ref[:, 0:HG_WIDTH] = finish(hg).astype(o_ref.dtype)
        next(rw)
        o_ref[:, HG_WIDTH:D_MODEL] = finish(rw).astype(o_ref.dtype)
        project(IN_COLS * D_MODEL // MXU_WIDTH ** 2)

    @pl.when((t % 2 == 0) & (t > 0))
    def _():
        step(proj_a, proj_b)

    @pl.when(t % 2 == 1)
    def _():
        step(proj_b, proj_a)


def _cast_block_rows(rows, steps):
    for block in range(BF16_SUBLANES, rows + 1, BF16_SUBLANES):
        if rows % block == 0 and rows // block <= steps:
            return block
    raise ValueError(f"cannot convert {rows} weight rows in {steps} grid steps")


def _mixers(x, norm1_w, w_in, lbp, hg_norm_w, mix, w0, a0, lora_w, g2, k_k, k_a, r_k, ln_w, ln_b,
            tril, seg, ffn_weights):
    batch, seq, _ = x.shape
    rows = MIX_BLOCK_CHUNKS * CHUNK
    nblk = seq // rows
    total = batch * nblk
    steps = total + 1
    const = lambda shape: pl.BlockSpec(shape, lambda t: (0, 0), pipeline_mode=pl.Buffered(1))
    vec = lambda n: const((1, n))

    def block_of(i):
        return i // nblk, i % nblk, 0

    def cast_spec(w):
        block = _cast_block_rows(w.shape[0], steps)
        last = w.shape[0] // block - 1
        return pl.BlockSpec((block, w.shape[1]), lambda t: (jnp.minimum(t, last), 0))

    cast_specs = [cast_spec(w) for w in ffn_weights]
    return pl.pallas_call(
        functools.partial(_mixer_kernel, nblk),
        grid=(steps,),
        in_specs=[
            pl.BlockSpec((None, rows, D_MODEL), lambda t: block_of(jnp.minimum(t, total - 1))),
            vec(D_MODEL), const((D_MODEL, IN_COLS)),
            const((2, HG_WIDTH)), vec(HG_WIDTH),
            vec(RW_COLS), vec(RW_WIDTH), vec(RW_WIDTH),
            const((LANES, 2 * RW_WIDTH)), const((LANES, RW_WIDTH)),
            vec(RW_WIDTH), vec(RW_WIDTH), vec(RW_WIDTH), vec(RW_WIDTH), vec(RW_WIDTH),
            const(tril.shape), const((MXU_WIDTH, MXU_WIDTH)),
        ] + cast_specs,
        out_specs=[pl.BlockSpec((None, rows, D_MODEL),
                                lambda t: block_of(jnp.maximum(t - 1, 0)))] + cast_specs,
        out_shape=[jax.ShapeDtypeStruct((batch, seq, D_MODEL), BF16)]
        + [jax.ShapeDtypeStruct(w.shape, BF16) for w in ffn_weights],
        scratch_shapes=[pltpu.VMEM((rows, IN_COLS), F32), pltpu.VMEM((rows, IN_COLS), F32),
                        pltpu.VMEM((HG_HEADS, HG_DK, HG_DK), F32),
                        pltpu.VMEM((PAIRS, RW_HEAD, LANES), F32),
                        pltpu.VMEM((1, RW_COLS), F32)],
        compiler_params=pltpu.CompilerParams(
            dimension_semantics=("arbitrary",), vmem_limit_bytes=VMEM_LIMIT),
        name="mixers",
    )(x, norm1_w, w_in, lbp, hg_norm_w, mix, w0, a0, lora_w, g2, k_k, k_a, r_k, ln_w, ln_b,
      tril, seg, *ffn_weights)


def _ffn_kernel(ntiles, x_ref, o_ref, wo_ref, n2_ref, w1_ref, w3_ref, w2_ref, fn_ref, out_ref,
                x2_scr):
    i = pl.program_id(0)

    def finalize():
        x2 = x2_scr[...]
        ms2 = jnp.mean(x2 * x2, axis=-1, keepdims=True)
        out_ref[...] = x2 * lax.rsqrt(ms2 + EPS) * fn_ref[...]

    @pl.when(i == 0)
    def _():
        x2_scr[...] = jnp.zeros_like(x2_scr)

    @pl.when(i < ntiles)
    def _():
        x1 = x_ref[...] + jnp.dot(o_ref[...], wo_ref[...], preferred_element_type=F32)
        h = _rmsnorm_bf16(x1, n2_ref[...])
        a = jnp.dot(h, w1_ref[...], preferred_element_type=F32)
        finalize()
        b = jnp.dot(h, w3_ref[...], preferred_element_type=F32)
        hm = (_silu(a) * b).astype(BF16)
        x2_scr[...] = x1 + jnp.dot(hm, w2_ref[...], preferred_element_type=F32)

    @pl.when(i == ntiles)
    def _():
        finalize()


def _ffn(x2, o_mix, w_out, norm2_w, w1, w3, w2, final_w, tm):
    rows = x2.shape[0]
    ntiles = rows // tm
    const = lambda r, c_: pl.BlockSpec((r, c_), lambda i: (0, 0), pipeline_mode=pl.Buffered(1))
    tile_in = pl.BlockSpec((tm, D_MODEL), lambda i: (jnp.minimum(i, ntiles - 1), 0))
    return pl.pallas_call(
        functools.partial(_ffn_kernel, ntiles),
        grid=(ntiles + 1,),
        in_specs=[
            tile_in, tile_in,
            const(D_MODEL, D_MODEL), const(1, D_MODEL),
            const(D_MODEL, D_FF), const(D_MODEL, D_FF), const(D_FF, D_MODEL),
            const(1, D_MODEL),
        ],
        out_specs=pl.BlockSpec((tm, D_MODEL), lambda i: (jnp.maximum(i - 1, 0), 0)),
        out_shape=jax.ShapeDtypeStruct((rows, D_MODEL), F32),
        scratch_shapes=[pltpu.VMEM((tm, D_MODEL), F32)],
        compiler_params=pltpu.CompilerParams(
            dimension_semantics=("arbitrary",), vmem_limit_bytes=VMEM_LIMIT),
        name="outproj_ffn",
    )(x2, o_mix, w_out, norm2_w, w1, w3, w2, final_w)


def kernel(x, norm1_w, w_in, hg_lower_bounds, hg_norm_w, rw_mix, rw_w0, rw_w2, rw_a0, rw_a2,
           rw_g2, rw_k_k, rw_k_a, rw_r_k, rw_ln_w, rw_ln_b, w_out, norm2_w, ffn_w1, ffn_w3,
           ffn_w2, final_norm_w):
    batch, seq, _ = x.shape
    rows = batch * seq
    row = lambda z: z.reshape(1, -1)

    tril = jnp.kron(jnp.eye(MXU_WIDTH // CHUNK, dtype=BF16),
                    jnp.tril(jnp.ones((CHUNK, CHUNK), BF16)))
    zero = jnp.zeros((64, RW_WIDTH), F32)
    lora_w = jnp.concatenate([jnp.concatenate([rw_w2[0], zero], axis=1),
                              jnp.concatenate([zero, rw_a2[0]], axis=1)], axis=0).astype(BF16)
    head_id = jnp.arange(MXU_WIDTH) // RW_HEAD
    seg = (head_id[:, None] == head_id[None, :]).astype(BF16)
    o_mix, wo_bf, w1_bf, w3_bf, w2_bf = _mixers(
        x, row(norm1_w[0]), w_in[0].astype(BF16), hg_lower_bounds, row(hg_norm_w[0]),
        row(rw_mix[0]), row(rw_w0[0]), row(rw_a0[0]), lora_w, rw_g2[0].astype(BF16),
        row(rw_k_k[0]), row(rw_k_a[0]), row(rw_r_k[0]), row(rw_ln_w[0]), row(rw_ln_b[0]), tril, seg,
        (w_out[0], ffn_w1[0], ffn_w3[0], ffn_w2[0]))

    out = _ffn(x.reshape(rows, D_MODEL), o_mix.reshape(rows, D_MODEL), wo_bf, row(norm2_w[0]),
               w1_bf, w3_bf, w2_bf, row(final_norm_w), tm=512)
    return out.reshape(batch, seq, D_MODEL)
```

```python
import functools
import math

import jax
import jax.numpy as jnp
from jax import lax
from jax.experimental import pallas as pl
from jax.experimental.pallas import tpu as pltpu

F32 = jnp.float32
BF16 = jnp.bfloat16

D_MODEL = 1024
CHUNK = 64
EPS = 1e-6
HG_WIDTH = 512
HG_DK = 128
HG_HEADS = 4
RW_WIDTH = 512
RW_HEAD = 64
RW_HEADS = 8
RW_GN_EPS = 64e-5
HG_COLS = 4 * HG_WIDTH
RW_LORA = 64
RW_COLS = 3 * RW_WIDTH + 2 * RW_LORA + 128
IN_COLS = HG_COLS + RW_COLS
D_FF = 2816
LANES = 128
BF16_SUBLANES = 16
MXU_WIDTH = 256
PAIRS = RW_WIDTH // LANES
MIX_BLOCK_CHUNKS = 8
PROLOGUE_TILES = 20
PACE = 8
VMEM_LIMIT = 56 * 1024 * 1024


def _dot(a, b):
    return jnp.dot(a.astype(BF16), b.astype(BF16), preferred_element_type=F32)


def _dot_nt(a, b):
    return jnp.dot(a.astype(BF16), b.astype(F32).T.astype(BF16), preferred_element_type=F32)


def _dot_tn(a, b):
    return jnp.dot(a.T.astype(BF16), b.astype(BF16), preferred_element_type=F32)


def _cumsum_rows(tril, x):
    n = tril.shape[0]
    hi = x.astype(BF16)
    lo = (x - hi.astype(F32)).astype(BF16)
    return jnp.concatenate(
        [jnp.dot(tril, hi[r:r + n], preferred_element_type=F32)
         + jnp.dot(tril, lo[r:r + n], preferred_element_type=F32)
         for r in range(0, x.shape[0], n)], axis=0)


def _head_sums(x, seg):
    w = seg.shape[0]
    return jnp.concatenate([_dot(x[:, n:n + w], seg) for n in range(0, x.shape[1], w)], axis=1)


def _sigmoid(x):
    return 0.5 * jnp.tanh(0.5 * x) + 0.5


def _silu(x):
    return x * _sigmoid(x)


def _rmsnorm_bf16(x, w):
    ms = jnp.mean(x * x, axis=-1, keepdims=True)
    return (x * lax.rsqrt(ms + EPS) * w).astype(BF16)


def _hgrn2_block(p_ref, lbp_ref, nw_ref, tril, st_ref, paced):
    nchunk = p_ref.shape[0] // CHUNK
    lbp = lbp_ref[...]
    m = jnp.maximum(lbp[0:1], lbp[1:2])
    e0 = jnp.exp(lbp[0:1] - m)
    e1 = jnp.exp(lbp[1:2] - m)
    lb = e0 / (e0 + e1)

    q = p_ref[:, 0:HG_WIDTH]
    f = p_ref[:, HG_WIDTH:2 * HG_WIDTH]
    v = p_ref[:, 2 * HG_WIDTH:3 * HG_WIDTH]
    g = p_ref[:, 3 * HG_WIDTH:4 * HG_WIDTH]
    qf = _silu(q)
    fs = lb + (1.0 - lb) * _sigmoid(f)
    kf = 1.0 - fs
    b = _cumsum_rows(tril, jnp.log(fs))
    gate = _silu(g) * nw_ref[...]

    row = lax.broadcasted_iota(jnp.int32, (CHUNK, CHUNK), 0)
    col = lax.broadcasted_iota(jnp.int32, (CHUNK, CHUNK), 1)
    causal = col <= row

    qs, ks, q0, kh, vv, dec = ([] for _ in range(6))
    for c in range(nchunk):
        rs = slice(c * CHUNK, (c + 1) * CHUNK)
        b_c = b[rs]
        b_mid = b_c[CHUNK // 2:CHUNK // 2 + 1]
        b_last = b_c[CHUNK - 1:CHUNK]
        qs_c = qf[rs] * jnp.exp(b_c - b_mid)
        ks_c = kf[rs] * jnp.exp(b_mid - b_c)
        q0_c = qs_c * jnp.exp(b_mid)
        kh_c = ks_c * jnp.exp(b_last - b_mid)
        dec_c = jnp.exp(b_last)
        v_c = v[rs]
        for h in range(HG_HEADS):
            sl = slice(h * HG_DK, (h + 1) * HG_DK)
            for lst, val in ((qs, qs_c), (ks, ks_c), (q0, q0_c), (kh, kh_c), (vv, v_c),
                             (dec, dec_c)):
                lst.append(val[:, sl])
    items = range(nchunk * HG_HEADS)
    yield
    scores = [jnp.where(causal, paced(_dot_nt(qs[i], ks[i])), 0.0) for i in items]
    kv = [paced(_dot_tn(vv[i], kh[i])) for i in items]
    o_in = [paced(_dot(scores[i], vv[i])) for i in items]
    yield
    st = [st_ref[h] for h in range(HG_HEADS)]
    o_rows = []
    for c in range(nchunk):
        o_c = []
        for h in range(HG_HEADS):
            i = c * HG_HEADS + h
            o = o_in[i] + paced(_dot_nt(q0[i], st[h]))
            st[h] = st[h] * dec[i] + kv[i]
            o_c.append(o * lax.rsqrt(jnp.mean(o * o, axis=-1, keepdims=True) + EPS))
        o_rows.append(jnp.concatenate(o_c, axis=1))
    for h in range(HG_HEADS):
        st_ref[h] = st[h]
    return jnp.concatenate(o_rows, axis=0) * gate


def _rwkv7_block(p_ref, mix_ref, w0_ref, a0_ref, lora_ref, g2_ref, kk_ref, ka_ref, rk_ref,
                 lnw_ref, lnb_ref, tril, seg, s_ref, prev_ref, paced):
    rows = p_ref.shape[0]
    nchunk = rows // CHUNK
    p = p_ref[...]
    rowi = lax.broadcasted_iota(jnp.int32, (rows, RW_COLS), 0)
    shifted = jnp.where(rowi == 0, prev_ref[...], pltpu.roll(p, 1, axis=0))
    prev_ref[...] = p[rows - 1:rows]
    pf = p + (shifted - p) * mix_ref[...]

    r = pf[:, 0:RW_WIDTH]
    k = pf[:, RW_WIDTH:2 * RW_WIDTH]
    v = pf[:, 2 * RW_WIDTH:3 * RW_WIDTH]
    d_wa = pf[:, 3 * RW_WIDTH:3 * RW_WIDTH + LANES]
    d_g = pf[:, 3 * RW_WIDTH + LANES:3 * RW_WIDTH + 2 * LANES]
    lane_r = lax.broadcasted_iota(jnp.int32, (rows, LANES), 1)
    lora = _dot(jnp.where(lane_r < RW_HEAD, jnp.tanh(d_wa), d_wa), lora_ref[...])
    u = w0_ref[...] + lora[:, 0:RW_WIDTH]
    ld = -math.exp(-0.5) * _sigmoid(u)
    alpha = _sigmoid(a0_ref[...] + lora[:, RW_WIDTH:2 * RW_WIDTH])
    gate = _dot(_sigmoid(d_g), g2_ref[...])

    kk = k * kk_ref[...]
    n2 = _head_sums(kk * kk, seg)
    kk = kk * jnp.minimum(lax.rsqrt(n2), 1e12)
    km = k * (1.0 + (alpha - 1.0) * ka_ref[...])
    cum = _cumsum_rows(tril, ld)
    cum_prev = cum - ld
    kka = kk * alpha
    yield

    lane = lax.broadcasted_iota(jnp.int32, (CHUNK, LANES), 1)
    m0 = lane < RW_HEAD
    t_i = lax.broadcasted_iota(jnp.int32, (CHUNK, LANES), 0)
    s_i = lane & (RW_HEAD - 1)
    strict = s_i < t_i
    incl = s_i <= t_i
    eye = (s_i == t_i).astype(F32)
    levels = [((t_i >> (j + 1)) == (s_i >> (j + 1))) & ((t_i >> j) != (s_i >> j)) & strict
              for j in range(6)]
    r128 = lax.broadcasted_iota(jnp.int32, (LANES, LANES), 0)
    c128 = lax.broadcasted_iota(jnp.int32, (LANES, LANES), 1)
    same_head = (r128 >= RW_HEAD) == (c128 >= RW_HEAD)
    diag128 = r128 == c128
    zeros = jnp.zeros((CHUNK, LANES), F32)

    def bd(x):
        return jnp.concatenate([jnp.where(m0, x, 0.0), jnp.where(m0, 0.0, x)], axis=0)

    ac, bc, kc, rc, a0, r0, bh, kh, vv, gam = ([] for _ in range(10))
    for c in range(nchunk):
        rs = slice(c * CHUNK, (c + 1) * CHUNK)
        cum_c = cum[rs]
        c_mid = cum_c[CHUNK // 2 - 1:CHUNK // 2]
        c_last = cum_c[CHUNK - 1:CHUNK]
        e_mid = jnp.exp(c_mid)
        e_tail = jnp.exp(c_last - c_mid)
        gamma = jnp.exp(c_last)
        dn = jnp.exp(c_mid - cum_c)
        ac_c = -kk[rs] * jnp.exp(cum_prev[rs] - c_mid)
        bc_c = kka[rs] * dn
        kc_c = km[rs] * dn
        rc_c = r[rs] * jnp.exp(cum_c - c_mid)
        a0_c = ac_c * e_mid
        r0_c = rc_c * e_mid
        bh_c = bc_c * e_tail
        kh_c = kc_c * e_tail
        v_c = v[rs]
        for pr in range(PAIRS):
            sl = slice(pr * LANES, (pr + 1) * LANES)
            for lst, val in ((ac, ac_c), (bc, bc_c), (kc, kc_c), (rc, rc_c), (a0, a0_c), (r0, r0_c),
                             (bh, bh_c), (kh, kh_c), (vv, v_c), (gam, gamma)):
                lst.append(val[:, sl])
    chains = range(nchunk * PAIRS)
    yield

    gram = [paced(_dot_nt(jnp.concatenate([ac[i], rc[i]], axis=0),
                          jnp.concatenate([bd(bc[i]), bd(kc[i])], axis=0)))
            for i in chains]
    n_ab = [jnp.where(strict, gram[i][0:CHUNK, 0:LANES], 0.0) for i in chains]
    m_ak = [jnp.where(strict, gram[i][0:CHUNK, LANES:2 * LANES], 0.0) for i in chains]
    p_rb = [jnp.where(incl, gram[i][CHUNK:2 * CHUNK, 0:LANES], 0.0) for i in chains]
    p_rk = [jnp.where(incl, gram[i][CHUNK:2 * CHUNK, LANES:2 * LANES], 0.0) for i in chains]
    xv = [paced(_dot(jnp.concatenate([m_ak[i], p_rk[i]], axis=0), bd(vv[i]))) for i in chains]
    x1 = [xv[i][0:CHUNK] for i in chains]
    yield
    tinv = [eye + jnp.where(levels[0], n_ab[i], 0.0) for i in chains]
    for j in range(1, 6):
        fj = [paced(_dot(jnp.where(levels[j], n_ab[i], 0.0), bd(tinv[i]))) for i in chains]
        tinv = [tinv[i] + paced(_dot(tinv[i], bd(fj[i]))) for i in chains]
        if j in (2, 4):
            yield
    uw = [paced(_dot(tinv[i], jnp.concatenate([bd(x1[i]), bd(a0[i])], axis=1)))
          for i in chains]
    yield
    u_loc = [uw[i][:, 0:LANES] for i in chains]
    w_eff = [uw[i][:, LANES:2 * LANES] for i in chains]
    yq = [paced(_dot(p_rb[i], jnp.concatenate([bd(u_loc[i]), bd(w_eff[i])], axis=1)))
          for i in chains]
    y_loc = [yq[i][:, 0:LANES] + xv[i][CHUNK:2 * CHUNK] for i in chains]
    gd = [paced(_dot_tn(jnp.concatenate([jnp.concatenate([u_loc[i], w_eff[i]], axis=1),
                                         jnp.concatenate([vv[i], zeros], axis=1)], axis=0),
                        jnp.concatenate([bh[i], kh[i]], axis=0))) for i in chains]
    d_t = [jnp.where(m0, gd[i][0:CHUNK], gd[i][CHUNK:LANES]) for i in chains]
    g_t = [jnp.where(same_head, gd[i][LANES:2 * LANES], 0.0)
           + jnp.where(diag128, jnp.broadcast_to(gam[i], (LANES, LANES)), 0.0) for i in chains]
    q_eff = [r0[i] + yq[i][:, LANES:2 * LANES] for i in chains]

    st = [s_ref[pr] for pr in range(PAIRS)]
    y_rows = []
    for c in range(nchunk):
        st_in = st
        st = [paced(_dot(st_in[pr], g_t[c * PAIRS + pr])) + d_t[c * PAIRS + pr]
              for pr in range(PAIRS)]
        y_rows.append(jnp.concatenate(
            [paced(_dot_nt(q_eff[c * PAIRS + pr], bd(st_in[pr]))) + y_loc[c * PAIRS + pr]
             for pr in range(PAIRS)], axis=1))
    for pr in range(PAIRS):
        s_ref[pr] = st[pr]
    y = jnp.concatenate(y_rows, axis=0)

    inv_n = 1.0 / RW_HEAD
    mu = _head_sums(y, seg) * inv_n
    yc = y - mu
    var = _head_sums(yc * yc, seg) * inv_n
    yn = yc * lax.rsqrt(var + RW_GN_EPS) * lnw_ref[...] + lnb_ref[...]
    bonus = _head_sums(r * km * rk_ref[...], seg) * v
    return (yn + bonus) * gate


def _mixer_kernel(nblk, x_ref, n1w_ref, win_ref, lbp_ref, hgnw_ref, mix_ref, w0_ref, a0_ref, lora_ref,
                  g2_ref, kk_ref, ka_ref, rk_ref, lnw_ref, lnb_ref, tril_ref, seg_ref,
                  wo_ref, w1_ref, w3_ref, w2_ref, o_ref, wo_bf, w1_bf, w3_bf, w2_bf,
                  proj_a, proj_b, hst_ref, s_ref, prev_ref):
    t = pl.program_id(0)
    for src, dst in ((wo_ref, wo_bf), (w1_ref, w1_bf), (w3_ref, w3_bf), (w2_ref, w2_bf)):
        dst[...] = src[...].astype(BF16)

    @pl.when(t == 0)
    def _():
        proj_a[...] = jnp.dot(_rmsnorm_bf16(x_ref[...], n1w_ref[...]), win_ref[...],
                              preferred_element_type=F32)

    @pl.when((t > 0) & ((t - 1) % nblk == 0))
    def _():
        hst_ref[...] = jnp.zeros_like(hst_ref)
        s_ref[...] = jnp.zeros_like(s_ref)
        prev_ref[...] = jnp.zeros_like(prev_ref)

    def finish(gen):
        try:
            next(gen)
        except StopIteration as done:
            return done.value
        raise AssertionError("mixer generator has more phases than the schedule below")

    def step(p_out, p_in):
        h = _rmsnorm_bf16(x_ref[...], n1w_ref[...])
        tiles = iter([(n, k) for n in range(0, IN_COLS, MXU_WIDTH)
                      for k in range(0, D_MODEL, MXU_WIDTH)])
        partial = [None]

        def project(count):
            for n, k in [t for _, t in zip(range(count), tiles)]:
                part = jnp.dot(h[:, k:k + MXU_WIDTH], win_ref[k:k + MXU_WIDTH, n:n + MXU_WIDTH],
                               preferred_element_type=F32)
                partial[0] = part if k == 0 else partial[0] + part
                if k + MXU_WIDTH == D_MODEL:
                    p_out[:, n:n + MXU_WIDTH] = partial[0]

        calls = [0]

        def paced(value):
            calls[0] += 1
            if calls[0] % PACE == 0:
                project(1)
            return value

        tril = tril_ref[...]
        hg = _hgrn2_block(p_in.at[:, 0:HG_COLS], lbp_ref, hgnw_ref, tril, hst_ref, paced)
        rw = _rwkv7_block(p_in.at[:, HG_COLS:IN_COLS], mix_ref, w0_ref, a0_ref, lora_ref, g2_ref,
                          kk_ref, ka_ref, rk_ref, lnw_ref, lnb_ref, tril, seg_ref[...], s_ref,
                          prev_ref, paced)
        project(PROLOGUE_TILES // 2)
        next(rw)
        project(PROLOGUE_TILES - PROLOGUE_TILES // 2)
        next(rw)
        next(rw)
        next(hg)
        next(rw)
        next(hg)
        next(rw)
        o_ref[:, 0:HG_WIDTH] = finish(hg).astype(o_ref.dtype)
        next(rw)
        o_ref[:, HG_WIDTH:D_MODEL] = finish(rw).astype(o_ref.dtype)
        project(IN_COLS * D_MODEL // MXU_WIDTH ** 2)

    @pl.when((t % 2 == 0) & (t > 0))
    def _():
        step(proj_a, proj_b)

    @pl.when(t % 2 == 1)
    def _():
        step(proj_b, proj_a)


def _cast_block_rows(rows, steps):
    for block in range(BF16_SUBLANES, rows + 1, BF16_SUBLANES):
        if rows % block == 0 and rows // block <= steps:
            return block
    raise ValueError(f"cannot convert {rows} weight rows in {steps} grid steps")


def _mixers(x, norm1_w, w_in, lbp, hg_norm_w, mix, w0, a0, lora_w, g2, k_k, k_a, r_k, ln_w, ln_b,
            tril, seg, ffn_weights):
    batch, seq, _ = x.shape
    rows = MIX_BLOCK_CHUNKS * CHUNK
    nblk = seq // rows
    total = batch * nblk
    steps = total + 1
    const = lambda shape: pl.BlockSpec(shape, lambda t: (0, 0), pipeline_mode=pl.Buffered(1))
    vec = lambda n: const((1, n))

    def block_of(i):
        return i // nblk, i % nblk, 0

    def cast_spec(w):
        block = _cast_block_rows(w.shape[0], steps)
        last = w.shape[0] // block - 1
        return pl.BlockSpec((block, w.shape[1]), lambda t: (jnp.minimum(t, last), 0))

    cast_specs = [cast_spec(w) for w in ffn_weights]
    return pl.pallas_call(
        functools.partial(_mixer_kernel, nblk),
        grid=(steps,),
        in_specs=[
            pl.BlockSpec((None, rows, D_MODEL), lambda t: block_of(jnp.minimum(t, total - 1))),
            vec(D_MODEL), const((D_MODEL, IN_COLS)),
            const((2, HG_WIDTH)), vec(HG_WIDTH),
            vec(RW_COLS), vec(RW_WIDTH), vec(RW_WIDTH),
            const((LANES, 2 * RW_WIDTH)), const((LANES, RW_WIDTH)),
            vec(RW_WIDTH), vec(RW_WIDTH), vec(RW_WIDTH), vec(RW_WIDTH), vec(RW_WIDTH),
            const(tril.shape), const((MXU_WIDTH, MXU_WIDTH)),
        ] + cast_specs,
        out_specs=[pl.BlockSpec((None, rows, D_MODEL),
                                lambda t: block_of(jnp.maximum(t - 1, 0)))] + cast_specs,
        out_shape=[jax.ShapeDtypeStruct((batch, seq, D_MODEL), BF16)]
        + [jax.ShapeDtypeStruct(w.shape, BF16) for w in ffn_weights],
        scratch_shapes=[pltpu.VMEM((rows, IN_COLS), F32), pltpu.VMEM((rows, IN_COLS), F32),
                        pltpu.VMEM((HG_HEADS, HG_DK, HG_DK), F32),
                        pltpu.VMEM((PAIRS, RW_HEAD, LANES), F32),
                        pltpu.VMEM((1, RW_COLS), F32)],
        compiler_params=pltpu.CompilerParams(
            dimension_semantics=("arbitrary",), vmem_limit_bytes=VMEM_LIMIT),
        name="mixers",
    )(x, norm1_w, w_in, lbp, hg_norm_w, mix, w0, a0, lora_w, g2, k_k, k_a, r_k, ln_w, ln_b,
      tril, seg, *ffn_weights)


def _ffn_kernel(ntiles, x_ref, o_ref, wo_ref, n2_ref, w1_ref, w3_ref, w2_ref, fn_ref, out_ref,
                x2_scr):
    i = pl.program_id(0)

    def finalize():
        x2 = x2_scr[...]
        ms2 = jnp.mean(x2 * x2, axis=-1, keepdims=True)
        out_ref[...] = x2 * lax.rsqrt(ms2 + EPS) * fn_ref[...]

    @pl.when(i == 0)
    def _():
        x2_scr[...] = jnp.zeros_like(x2_scr)

    @pl.when(i < ntiles)
    def _():
        x1 = x_ref[...] + jnp.dot(o_ref[...], wo_ref[...], preferred_element_type=F32)
        h = _rmsnorm_bf16(x1, n2_ref[...])
        a = jnp.dot(h, w1_ref[...], preferred_element_type=F32)
        finalize()
        b = jnp.dot(h, w3_ref[...], preferred_element_type=F32)
        hm = (_silu(a) * b).astype(BF16)
        x2_scr[...] = x1 + jnp.dot(hm, w2_ref[...], preferred_element_type=F32)

    @pl.when(i == ntiles)
    def _():
        finalize()


def _ffn(x2, o_mix, w_out, norm2_w, w1, w3, w2, final_w, tm):
    rows = x2.shape[0]
    ntiles = rows // tm
    const = lambda r, c_: pl.BlockSpec((r, c_), lambda i: (0, 0), pipeline_mode=pl.Buffered(1))
    tile_in = pl.BlockSpec((tm, D_MODEL), lambda i: (jnp.minimum(i, ntiles - 1), 0))
    return pl.pallas_call(
        functools.partial(_ffn_kernel, ntiles),
        grid=(ntiles + 1,),
        in_specs=[
            tile_in, tile_in,
            const(D_MODEL, D_MODEL), const(1, D_MODEL),
            const(D_MODEL, D_FF), const(D_MODEL, D_FF), const(D_FF, D_MODEL),
            const(1, D_MODEL),
        ],
        out_specs=pl.BlockSpec((tm, D_MODEL), lambda i: (jnp.maximum(i - 1, 0), 0)),
        out_shape=jax.ShapeDtypeStruct((rows, D_MODEL), F32),
        scratch_shapes=[pltpu.VMEM((tm, D_MODEL), F32)],
        compiler_params=pltpu.CompilerParams(
            dimension_semantics=("arbitrary",), vmem_limit_bytes=VMEM_LIMIT),
        name="outproj_ffn",
    )(x2, o_mix, w_out, norm2_w, w1, w3, w2, final_w)


def kernel(x, norm1_w, w_in, hg_lower_bounds, hg_norm_w, rw_mix, rw_w0, rw_w2, rw_a0, rw_a2,
           rw_g2, rw_k_k, rw_k_a, rw_r_k, rw_ln_w, rw_ln_b, w_out, norm2_w, ffn_w1, ffn_w3,
           ffn_w2, final_norm_w):
    batch, seq, _ = x.shape
    rows = batch * seq
    row = lambda z: z.reshape(1, -1)

    tril = jnp.kron(jnp.eye(MXU_WIDTH // CHUNK, dtype=BF16),
                    jnp.tril(jnp.ones((CHUNK, CHUNK), BF16)))
    zero = jnp.zeros((RW_LORA, RW_WIDTH), F32)
    lora_w = jnp.concatenate([jnp.concatenate([rw_w2[0], zero], axis=1),
                              jnp.concatenate([zero, rw_a2[0]], axis=1)], axis=0).astype(BF16)
    head_id = jnp.arange(MXU_WIDTH) // RW_HEAD
    seg = (head_id[:, None] == head_id[None, :]).astype(BF16)
    o_mix, wo_bf, w1_bf, w3_bf, w2_bf = _mixers(
        x, row(norm1_w[0]), w_in[0].astype(BF16), hg_lower_bounds, row(hg_norm_w[0]),
        row(rw_mix[0]), row(rw_w0[0]), row(rw_a0[0]), lora_w, rw_g2[0].astype(BF16),
        row(rw_k_k[0]), row(rw_k_a[0]), row(rw_r_k[0]), row(rw_ln_w[0]), row(rw_ln_b[0]), tril, seg,
        (w_out[0], ffn_w1[0], ffn_w3[0], ffn_w2[0]))

    out = _ffn(x.reshape(rows, D_MODEL), o_mix.reshape(rows, D_MODEL), wo_bf, row(norm2_w[0]),
               w1_bf, w3_bf, w2_bf, row(final_norm_w), tm=512)
    return out.reshape(batch, seq, D_MODEL)
```

```python
import functools
import math

import jax
import jax.numpy as jnp
from jax import lax
from jax.experimental import pallas as pl
from jax.experimental.pallas import tpu as pltpu

F32 = jnp.float32
BF16 = jnp.bfloat16

D_MODEL = 1024
CHUNK = 64
EPS = 1e-6
HG_WIDTH = 512
HG_DK = 128
HG_HEADS = 4
RW_WIDTH = 512
RW_HEAD = 64
RW_HEADS = 8
RW_GN_EPS = 64e-5
HG_COLS = 4 * HG_WIDTH
RW_LORA = 64
RW_COLS = 3 * RW_WIDTH + 2 * RW_LORA + 128
IN_COLS = HG_COLS + RW_COLS
D_FF = 2816
LANES = 128
BF16_SUBLANES = 16
MXU_WIDTH = 256
PAIRS = RW_WIDTH // LANES
MIX_BLOCK_CHUNKS = 8
PROLOGUE_TILES = 20
PACE = 8
VMEM_LIMIT = 56 * 1024 * 1024


def _dot(a, b):
    return jnp.dot(a.astype(BF16), b.astype(BF16), preferred_element_type=F32)


def _dot_nt(a, b):
    return jnp.dot(a.astype(BF16), b.astype(F32).T.astype(BF16), preferred_element_type=F32)


def _dot_tn(a, b):
    return jnp.dot(a.T.astype(BF16), b.astype(BF16), preferred_element_type=F32)


def _cumsum_rows(tril, x):
    n = tril.shape[0]
    hi = x.astype(BF16)
    lo = (x - hi.astype(F32)).astype(BF16)
    return jnp.concatenate(
        [jnp.dot(tril, jnp.concatenate([hi[r:r + n], lo[r:r + n]], axis=0),
                 preferred_element_type=F32)
         for r in range(0, x.shape[0], n)], axis=0)


def _head_sums(x, seg):
    w = seg.shape[0]
    return jnp.concatenate([_dot(x[:, n:n + w], seg) for n in range(0, x.shape[1], w)], axis=1)


def _sigmoid(x):
    return 0.5 * jnp.tanh(0.5 * x) + 0.5


def _silu(x):
    return x * _sigmoid(x)


def _rmsnorm_bf16(x, w):
    ms = jnp.mean(x * x, axis=-1, keepdims=True)
    return (x * lax.rsqrt(ms + EPS) * w).astype(BF16)


def _hgrn2_block(p_ref, lbp_ref, nw_ref, tril, st_ref, paced):
    nchunk = p_ref.shape[0] // CHUNK
    lbp = lbp_ref[...]
    m = jnp.maximum(lbp[0:1], lbp[1:2])
    e0 = jnp.exp(lbp[0:1] - m)
    e1 = jnp.exp(lbp[1:2] - m)
    lb = e0 / (e0 + e1)

    q = p_ref[:, 0:HG_WIDTH]
    f = p_ref[:, HG_WIDTH:2 * HG_WIDTH]
    v = p_ref[:, 2 * HG_WIDTH:3 * HG_WIDTH]
    g = p_ref[:, 3 * HG_WIDTH:4 * HG_WIDTH]
    qf = _silu(q)
    fs = lb + (1.0 - lb) * _sigmoid(f)
    kf = 1.0 - fs
    b = _cumsum_rows(tril, jnp.log(fs))
    gate = _silu(g) * nw_ref[...]

    row = lax.broadcasted_iota(jnp.int32, (CHUNK, CHUNK), 0)
    col = lax.broadcasted_iota(jnp.int32, (CHUNK, CHUNK), 1)
    causal = col <= row

    qs, ks, q0, kh, vv, dec = ([] for _ in range(6))
    for c in range(nchunk):
        rs = slice(c * CHUNK, (c + 1) * CHUNK)
        b_c = b[rs]
        b_mid = b_c[CHUNK // 2:CHUNK // 2 + 1]
        b_last = b_c[CHUNK - 1:CHUNK]
        qs_c = qf[rs] * jnp.exp(b_c - b_mid)
        ks_c = kf[rs] * jnp.exp(b_mid - b_c)
        q0_c = qs_c * jnp.exp(b_mid)
        kh_c = ks_c * jnp.exp(b_last - b_mid)
        dec_c = jnp.exp(b_last)
        v_c = v[rs]
        for h in range(HG_HEADS):
            sl = slice(h * HG_DK, (h + 1) * HG_DK)
            for lst, val in ((qs, qs_c), (ks, ks_c), (q0, q0_c), (kh, kh_c), (vv, v_c),
                             (dec, dec_c)):
                lst.append(val[:, sl])
    items = range(nchunk * HG_HEADS)
    yield
    scores = [jnp.where(causal, paced(_dot_nt(qs[i], ks[i])), 0.0) for i in items]
    kv = [paced(_dot_tn(vv[i], kh[i])) for i in items]
    o_in = [paced(_dot(scores[i], vv[i])) for i in items]
    yield
    st = [st_ref[h] for h in range(HG_HEADS)]
    o_rows = []
    for c in range(nchunk):
        o_c = []
        for h in range(HG_HEADS):
            i = c * HG_HEADS + h
            o = o_in[i] + paced(_dot_nt(q0[i], st[h]))
            st[h] = st[h] * dec[i] + kv[i]
            o_c.append(o * lax.rsqrt(jnp.mean(o * o, axis=-1, keepdims=True) + EPS))
        o_rows.append(jnp.concatenate(o_c, axis=1))
    for h in range(HG_HEADS):
        st_ref[h] = st[h]
    return jnp.concatenate(o_rows, axis=0) * gate


def _rwkv7_block(p_ref, mix_ref, w0_ref, a0_ref, lora_ref, g2_ref, kk_ref, ka_ref, rk_ref,
                 lnw_ref, lnb_ref, tril, seg, s_ref, prev_ref, paced):
    rows = p_ref.shape[0]
    nchunk = rows // CHUNK
    p = p_ref[...]
    rowi = lax.broadcasted_iota(jnp.int32, (rows, RW_COLS), 0)
    shifted = jnp.where(rowi == 0, prev_ref[...], pltpu.roll(p, 1, axis=0))
    prev_ref[...] = p[rows - 1:rows]
    pf = p + (shifted - p) * mix_ref[...]

    r = pf[:, 0:RW_WIDTH]
    k = pf[:, RW_WIDTH:2 * RW_WIDTH]
    v = pf[:, 2 * RW_WIDTH:3 * RW_WIDTH]
    d_wa = pf[:, 3 * RW_WIDTH:3 * RW_WIDTH + LANES]
    d_g = pf[:, 3 * RW_WIDTH + LANES:3 * RW_WIDTH + 2 * LANES]
    lane_r = lax.broadcasted_iota(jnp.int32, (rows, LANES), 1)
    lora = _dot(jnp.where(lane_r < RW_HEAD, jnp.tanh(d_wa), d_wa), lora_ref[...])
    u = w0_ref[...] + lora[:, 0:RW_WIDTH]
    ld = -math.exp(-0.5) * _sigmoid(u)
    alpha = _sigmoid(a0_ref[...] + lora[:, RW_WIDTH:2 * RW_WIDTH])
    gate = _dot(_sigmoid(d_g), g2_ref[...])

    kk = k * kk_ref[...]
    n2 = _head_sums(kk * kk, seg)
    kk = kk * jnp.minimum(lax.rsqrt(n2), 1e12)
    km = k * (1.0 + (alpha - 1.0) * ka_ref[...])
    cum = _cumsum_rows(tril, ld)
    cum_prev = cum - ld
    kka = kk * alpha
    yield

    lane = lax.broadcasted_iota(jnp.int32, (CHUNK, LANES), 1)
    m0 = lane < RW_HEAD
    t_i = lax.broadcasted_iota(jnp.int32, (CHUNK, LANES), 0)
    s_i = lane & (RW_HEAD - 1)
    strict = s_i < t_i
    incl = s_i <= t_i
    eye = (s_i == t_i).astype(F32)
    levels = [((t_i >> (j + 1)) == (s_i >> (j + 1))) & ((t_i >> j) != (s_i >> j)) & strict
              for j in range(6)]
    r128 = lax.broadcasted_iota(jnp.int32, (LANES, LANES), 0)
    c128 = lax.broadcasted_iota(jnp.int32, (LANES, LANES), 1)
    same_head = (r128 >= RW_HEAD) == (c128 >= RW_HEAD)
    diag128 = r128 == c128
    zeros = jnp.zeros((CHUNK, LANES), F32)

    def bd(x):
        return jnp.concatenate([jnp.where(m0, x, 0.0), jnp.where(m0, 0.0, x)], axis=0)

    ac, bc, kc, rc, a0, r0, bh, kh, vv, gam = ([] for _ in range(10))
    for c in range(nchunk):
        rs = slice(c * CHUNK, (c + 1) * CHUNK)
        cum_c = cum[rs]
        c_mid = cum_c[CHUNK // 2 - 1:CHUNK // 2]
        c_last = cum_c[CHUNK - 1:CHUNK]
        e_mid = jnp.exp(c_mid)
        e_tail = jnp.exp(c_last - c_mid)
        gamma = jnp.exp(c_last)
        dn = jnp.exp(c_mid - cum_c)
        ac_c = -kk[rs] * jnp.exp(cum_prev[rs] - c_mid)
        bc_c = kka[rs] * dn
        kc_c = km[rs] * dn
        rc_c = r[rs] * jnp.exp(cum_c - c_mid)
        a0_c = ac_c * e_mid
        r0_c = rc_c * e_mid
        bh_c = bc_c * e_tail
        kh_c = kc_c * e_tail
        v_c = v[rs]
        for pr in range(PAIRS):
            sl = slice(pr * LANES, (pr + 1) * LANES)
            for lst, val in ((ac, ac_c), (bc, bc_c), (kc, kc_c), (rc, rc_c), (a0, a0_c), (r0, r0_c),
                             (bh, bh_c), (kh, kh_c), (vv, v_c), (gam, gamma)):
                lst.append(val[:, sl])
    chains = range(nchunk * PAIRS)
    yield

    gram = [paced(_dot_nt(jnp.concatenate([ac[i], rc[i]], axis=0),
                          jnp.concatenate([bd(bc[i]), bd(kc[i])], axis=0)))
            for i in chains]
    n_ab = [jnp.where(strict, gram[i][0:CHUNK, 0:LANES], 0.0) for i in chains]
    m_ak = [jnp.where(strict, gram[i][0:CHUNK, LANES:2 * LANES], 0.0) for i in chains]
    p_rb = [jnp.where(incl, gram[i][CHUNK:2 * CHUNK, 0:LANES], 0.0) for i in chains]
    p_rk = [jnp.where(incl, gram[i][CHUNK:2 * CHUNK, LANES:2 * LANES], 0.0) for i in chains]
    xv = [paced(_dot(jnp.concatenate([m_ak[i], p_rk[i]], axis=0), bd(vv[i]))) for i in chains]
    x1 = [xv[i][0:CHUNK] for i in chains]
    yield
    tinv = [eye + jnp.where(levels[0], n_ab[i], 0.0) for i in chains]
    for j in range(1, 6):
        fj = [paced(_dot(jnp.where(levels[j], n_ab[i], 0.0), bd(tinv[i]))) for i in chains]
        tinv = [tinv[i] + paced(_dot(tinv[i], bd(fj[i]))) for i in chains]
        if j in (2, 4):
            yield
    uw = [paced(_dot(tinv[i], jnp.concatenate([bd(x1[i]), bd(a0[i])], axis=1)))
          for i in chains]
    yield
    u_loc = [uw[i][:, 0:LANES] for i in chains]
    w_eff = [uw[i][:, LANES:2 * LANES] for i in chains]
    yq = [paced(_dot(p_rb[i], jnp.concatenate([bd(u_loc[i]), bd(w_eff[i])], axis=1)))
          for i in chains]
    y_loc = [yq[i][:, 0:LANES] + xv[i][CHUNK:2 * CHUNK] for i in chains]
    gd = [paced(_dot_tn(jnp.concatenate([jnp.concatenate([u_loc[i], w_eff[i]], axis=1),
                                         jnp.concatenate([vv[i], zeros], axis=1)], axis=0),
                        jnp.concatenate([bh[i], kh[i]], axis=0))) for i in chains]
    d_t = [jnp.where(m0, gd[i][0:CHUNK], gd[i][CHUNK:LANES]) for i in chains]
    g_t = [jnp.where(same_head, gd[i][LANES:2 * LANES], 0.0)
           + jnp.where(diag128, jnp.broadcast_to(gam[i], (LANES, LANES)), 0.0) for i in chains]
    q_eff = [r0[i] + yq[i][:, LANES:2 * LANES] for i in chains]

    st = [s_ref[pr] for pr in range(PAIRS)]
    y_rows = []
    for c in range(nchunk):
        st_in = st
        st = [paced(_dot(st_in[pr], g_t[c * PAIRS + pr])) + d_t[c * PAIRS + pr]
              for pr in range(PAIRS)]
        y_rows.append(jnp.concatenate(
            [paced(_dot_nt(q_eff[c * PAIRS + pr], bd(st_in[pr]))) + y_loc[c * PAIRS + pr]
             for pr in range(PAIRS)], axis=1))
    for pr in range(PAIRS):
        s_ref[pr] = st[pr]
    y = jnp.concatenate(y_rows, axis=0)

    inv_n = 1.0 / RW_HEAD
    mu = _head_sums(y, seg) * inv_n
    yc = y - mu
    var = _head_sums(yc * yc, seg) * inv_n
    yn = yc * lax.rsqrt(var + RW_GN_EPS) * lnw_ref[...] + lnb_ref[...]
    bonus = _head_sums(r * km * rk_ref[...], seg) * v
    return (yn + bonus) * gate


def _mixer_kernel(nblk, x_ref, n1w_ref, win_ref, lbp_ref, hgnw_ref, mix_ref, w0_ref, a0_ref, lora_ref,
                  g2_ref, kk_ref, ka_ref, rk_ref, lnw_ref, lnb_ref, tril_ref, seg_ref,
                  wo_ref, w1_ref, w3_ref, w2_ref, o_ref, wo_bf, w1_bf, w3_bf, w2_bf,
                  proj_a, proj_b, hst_ref, s_ref, prev_ref):
    t = pl.program_id(0)
    for src, dst in ((wo_ref, wo_bf), (w1_ref, w1_bf), (w3_ref, w3_bf), (w2_ref, w2_bf)):
        dst[...] = src[...].astype(BF16)

    @pl.when(t == 0)
    def _():
        proj_a[...] = jnp.dot(_rmsnorm_bf16(x_ref[...], n1w_ref[...]), win_ref[...],
                              preferred_element_type=F32)

    @pl.when((t > 0) & ((t - 1) % nblk == 0))
    def _():
        hst_ref[...] = jnp.zeros_like(hst_ref)
        s_ref[...] = jnp.zeros_like(s_ref)
        prev_ref[...] = jnp.zeros_like(prev_ref)

    def finish(gen):
        try:
            next(gen)
        except StopIteration as done:
            return done.value
        raise AssertionError("mixer generator has more phases than the schedule below")

    def step(p_out, p_in):
        h = _rmsnorm_bf16(x_ref[...], n1w_ref[...])
        tiles = iter([(n, k) for n in range(0, IN_COLS, MXU_WIDTH)
                      for k in range(0, D_MODEL, MXU_WIDTH)])
        partial = [None]

        def project(count):
            for n, k in [t for _, t in zip(range(count), tiles)]:
                part = jnp.dot(h[:, k:k + MXU_WIDTH], win_ref[k:k + MXU_WIDTH, n:n + MXU_WIDTH],
                               preferred_element_type=F32)
                partial[0] = part if k == 0 else partial[0] + part
                if k + MXU_WIDTH == D_MODEL:
                    p_out[:, n:n + MXU_WIDTH] = partial[0]

        calls = [0]

        def paced(value):
            calls[0] += 1
            if calls[0] % PACE == 0:
                project(1)
            return value

        tril = tril_ref[...]
        hg = _hgrn2_block(p_in.at[:, 0:HG_COLS], lbp_ref, hgnw_ref, tril, hst_ref, paced)
        rw = _rwkv7_block(p_in.at[:, HG_COLS:IN_COLS], mix_ref, w0_ref, a0_ref, lora_ref, g2_ref,
                          kk_ref, ka_ref, rk_ref, lnw_ref, lnb_ref, tril, seg_ref[...], s_ref,
                          prev_ref, paced)
        project(PROLOGUE_TILES // 2)
        next(rw)
        project(PROLOGUE_TILES - PROLOGUE_TILES // 2)
        next(rw)
        next(rw)
        next(hg)
        next(rw)
        next(hg)
        next(rw)
        o_ref[:, 0:HG_WIDTH] = finish(hg).astype(o_ref.dtype)
        next(rw)
        o_ref[:, HG_WIDTH:D_MODEL] = finish(rw).astype(o_ref.dtype)
        project(IN_COLS * D_MODEL // MXU_WIDTH ** 2)

    @pl.when((t % 2 == 0) & (t > 0))
    def _():
        step(proj_a, proj_b)

    @pl.when(t % 2 == 1)
    def _():
        step(proj_b, proj_a)


def _cast_block_rows(rows, steps):
    for block in range(BF16_SUBLANES, rows + 1, BF16_SUBLANES):
        if rows % block == 0 and rows // block <= steps:
            return block
    raise ValueError(f"cannot convert {rows} weight rows in {steps} grid steps")


def _mixers(x, norm1_w, w_in, lbp, hg_norm_w, mix, w0, a0, lora_w, g2, k_k, k_a, r_k, ln_w, ln_b,
            tril, seg, ffn_weights):
    batch, seq, _ = x.shape
    rows = MIX_BLOCK_CHUNKS * CHUNK
    nblk = seq // rows
    total = batch * nblk
    steps = total + 1
    const = lambda shape: pl.BlockSpec(shape, lambda t: (0, 0), pipeline_mode=pl.Buffered(1))
    vec = lambda n: const((1, n))

    def block_of(i):
        return i // nblk, i % nblk, 0

    def cast_spec(w):
        block = _cast_block_rows(w.shape[0], steps)
        last = w.shape[0] // block - 1
        return pl.BlockSpec((block, w.shape[1]), lambda t: (jnp.minimum(t, last), 0))

    cast_specs = [cast_spec(w) for w in ffn_weights]
    return pl.pallas_call(
        functools.partial(_mixer_kernel, nblk),
        grid=(steps,),
        in_specs=[
            pl.BlockSpec((None, rows, D_MODEL), lambda t: block_of(jnp.minimum(t, total - 1))),
            vec(D_MODEL), const((D_MODEL, IN_COLS)),
            const((2, HG_WIDTH)), vec(HG_WIDTH),
            vec(RW_COLS), vec(RW_WIDTH), vec(RW_WIDTH),
            const((LANES, 2 * RW_WIDTH)), const((LANES, RW_WIDTH)),
            vec(RW_WIDTH), vec(RW_WIDTH), vec(RW_WIDTH), vec(RW_WIDTH), vec(RW_WIDTH),
            const(tril.shape), const((MXU_WIDTH, MXU_WIDTH)),
        ] + cast_specs,
        out_specs=[pl.BlockSpec((None, rows, D_MODEL),
                                lambda t: block_of(jnp.maximum(t - 1, 0)))] + cast_specs,
        out_shape=[jax.ShapeDtypeStruct((batch, seq, D_MODEL), BF16)]
        + [jax.ShapeDtypeStruct(w.shape, BF16) for w in ffn_weights],
        scratch_shapes=[pltpu.VMEM((rows, IN_COLS), F32), pltpu.VMEM((rows, IN_COLS), F32),
                        pltpu.VMEM((HG_HEADS, HG_DK, HG_DK), F32),
                        pltpu.VMEM((PAIRS, RW_HEAD, LANES), F32),
                        pltpu.VMEM((1, RW_COLS), F32)],
        compiler_params=pltpu.CompilerParams(
            dimension_semantics=("arbitrary",), vmem_limit_bytes=VMEM_LIMIT),
        name="mixers",
    )(x, norm1_w, w_in, lbp, hg_norm_w, mix, w0, a0, lora_w, g2, k_k, k_a, r_k, ln_w, ln_b,
      tril, seg, *ffn_weights)


def _ffn_kernel(ntiles, x_ref, o_ref, wo_ref, n2_ref, w1_ref, w3_ref, w2_ref, fn_ref, out_ref,
                x2_scr):
    i = pl.program_id(0)

    def finalize():
        x2 = x2_scr[...]
        ms2 = jnp.mean(x2 * x2, axis=-1, keepdims=True)
        out_ref[...] = x2 * lax.rsqrt(ms2 + EPS) * fn_ref[...]

    @pl.when(i == 0)
    def _():
        x2_scr[...] = jnp.zeros_like(x2_scr)

    @pl.when(i < ntiles)
    def _():
        x1 = x_ref[...] + jnp.dot(o_ref[...], wo_ref[...], preferred_element_type=F32)
        h = _rmsnorm_bf16(x1, n2_ref[...])
        a = jnp.dot(h, w1_ref[...], preferred_element_type=F32)
        finalize()
        b = jnp.dot(h, w3_ref[...], preferred_element_type=F32)
        hm = (_silu(a) * b).astype(BF16)
        x2_scr[...] = x1 + jnp.dot(hm, w2_ref[...], preferred_element_type=F32)

    @pl.when(i == ntiles)
    def _():
        finalize()


def _ffn(x2, o_mix, w_out, norm2_w, w1, w3, w2, final_w, tm):
    rows = x2.shape[0]
    ntiles = rows // tm
    const = lambda r, c_: pl.BlockSpec((r, c_), lambda i: (0, 0), pipeline_mode=pl.Buffered(1))
    tile_in = pl.BlockSpec((tm, D_MODEL), lambda i: (jnp.minimum(i, ntiles - 1), 0))
    return pl.pallas_call(
        functools.partial(_ffn_kernel, ntiles),
        grid=(ntiles + 1,),
        in_specs=[
            tile_in, tile_in,
            const(D_MODEL, D_MODEL), const(1, D_MODEL),
            const(D_MODEL, D_FF), const(D_MODEL, D_FF), const(D_FF, D_MODEL),
            const(1, D_MODEL),
        ],
        out_specs=pl.BlockSpec((tm, D_MODEL), lambda i: (jnp.maximum(i - 1, 0), 0)),
        out_shape=jax.ShapeDtypeStruct((rows, D_MODEL), F32),
        scratch_shapes=[pltpu.VMEM((tm, D_MODEL), F32)],
        compiler_params=pltpu.CompilerParams(
            dimension_semantics=("arbitrary",), vmem_limit_bytes=VMEM_LIMIT),
        name="outproj_ffn",
    )(x2, o_mix, w_out, norm2_w, w1, w3, w2, final_w)


def kernel(x, norm1_w, w_in, hg_lower_bounds, hg_norm_w, rw_mix, rw_w0, rw_w2, rw_a0, rw_a2,
           rw_g2, rw_k_k, rw_k_a, rw_r_k, rw_ln_w, rw_ln_b, w_out, norm2_w, ffn_w1, ffn_w3,
           ffn_w2, final_norm_w):
    batch, seq, _ = x.shape
    rows = batch * seq
    row = lambda z: z.reshape(1, -1)

    tril = jnp.kron(jnp.eye(MXU_WIDTH // 2 // CHUNK, dtype=BF16),
                    jnp.tril(jnp.ones((CHUNK, CHUNK), BF16)))
    tril = jnp.concatenate([tril, tril], axis=1)
    zero = jnp.zeros((RW_LORA, RW_WIDTH), F32)
    lora_w = jnp.concatenate([jnp.concatenate([rw_w2[0], zero], axis=1),
                              jnp.concatenate([zero, rw_a2[0]], axis=1)], axis=0).astype(BF16)
    head_id = jnp.arange(MXU_WIDTH) // RW_HEAD
    seg = (head_id[:, None] == head_id[None, :]).astype(BF16)
    o_mix, wo_bf, w1_bf, w3_bf, w2_bf = _mixers(
        x, row(norm1_w[0]), w_in[0].astype(BF16), hg_lower_bounds, row(hg_norm_w[0]),
        row(rw_mix[0]), row(rw_w0[0]), row(rw_a0[0]), lora_w, rw_g2[0].astype(BF16),
        row(rw_k_k[0]), row(rw_k_a[0]), row(rw_r_k[0]), row(rw_ln_w[0]), row(rw_ln_b[0]), tril, seg,
        (w_out[0], ffn_w1[0], ffn_w3[0], ffn_w2[0]))

    out = _ffn(x.reshape(rows, D_MODEL), o_mix.reshape(rows, D_MODEL), wo_bf, row(norm2_w[0]),
               w1_bf, w3_bf, w2_bf, row(final_norm_w), tm=512)
    return out.reshape(batch, seq, D_MODEL)
```

```python
import functools
import math

import jax
import jax.numpy as jnp
from jax import lax
from jax.experimental import pallas as pl
from jax.experimental.pallas import tpu as pltpu

F32 = jnp.float32
BF16 = jnp.bfloat16

D_MODEL = 1024
CHUNK = 64
EPS = 1e-6
HG_WIDTH = 512
HG_DK = 128
HG_HEADS = 4
RW_WIDTH = 512
RW_HEAD = 64
RW_HEADS = 8
RW_GN_EPS = 64e-5
HG_COLS = 4 * HG_WIDTH
RW_LORA = 64
RW_COLS = 3 * RW_WIDTH + 2 * RW_LORA + 128
IN_COLS = HG_COLS + RW_COLS
D_FF = 2816
LANES = 128
BF16_SUBLANES = 16
MXU_WIDTH = 256
PAIRS = RW_WIDTH // LANES
MIX_BLOCK_CHUNKS = 8
PROLOGUE_TILES = 20
PACE = 8
TAIL_BURST = 2
VMEM_LIMIT = 56 * 1024 * 1024


def _dot(a, b):
    return jnp.dot(a.astype(BF16), b.astype(BF16), preferred_element_type=F32)


def _dot_nt(a, b):
    return jnp.dot(a.astype(BF16), b.astype(F32).T.astype(BF16), preferred_element_type=F32)


def _dot_tn(a, b):
    return jnp.dot(a.T.astype(BF16), b.astype(BF16), preferred_element_type=F32)


def _cumsum_rows(tril, x):
    n = tril.shape[0]
    hi = x.astype(BF16)
    lo = (x - hi.astype(F32)).astype(BF16)
    return jnp.concatenate(
        [jnp.dot(tril, jnp.concatenate([hi[r:r + n], lo[r:r + n]], axis=0),
                 preferred_element_type=F32)
         for r in range(0, x.shape[0], n)], axis=0)


def _head_sums(x, seg):
    w = seg.shape[0]
    return jnp.concatenate([_dot(x[:, n:n + w], seg) for n in range(0, x.shape[1], w)], axis=1)


def _sigmoid(x):
    return 0.5 * jnp.tanh(0.5 * x) + 0.5


def _silu(x):
    return x * _sigmoid(x)


def _rmsnorm_bf16(x, w):
    ms = jnp.mean(x * x, axis=-1, keepdims=True)
    return (x * lax.rsqrt(ms + EPS) * w).astype(BF16)


def _hgrn2_block(p_ref, lbp_ref, nw_ref, tril, st_ref, paced):
    nchunk = p_ref.shape[0] // CHUNK
    lbp = lbp_ref[...]
    m = jnp.maximum(lbp[0:1], lbp[1:2])
    e0 = jnp.exp(lbp[0:1] - m)
    e1 = jnp.exp(lbp[1:2] - m)
    lb = e0 / (e0 + e1)

    q = p_ref[:, 0:HG_WIDTH]
    f = p_ref[:, HG_WIDTH:2 * HG_WIDTH]
    v = p_ref[:, 2 * HG_WIDTH:3 * HG_WIDTH]
    g = p_ref[:, 3 * HG_WIDTH:4 * HG_WIDTH]
    qf = _silu(q)
    fs = lb + (1.0 - lb) * _sigmoid(f)
    kf = 1.0 - fs
    b = _cumsum_rows(tril, jnp.log(fs))
    gate = _silu(g) * nw_ref[...]

    row = lax.broadcasted_iota(jnp.int32, (CHUNK, CHUNK), 0)
    col = lax.broadcasted_iota(jnp.int32, (CHUNK, CHUNK), 1)
    causal = col <= row

    qs, ks, q0, kh, vv, dec = ([] for _ in range(6))
    for c in range(nchunk):
        rs = slice(c * CHUNK, (c + 1) * CHUNK)
        b_c = b[rs]
        b_mid = b_c[CHUNK // 2:CHUNK // 2 + 1]
        b_last = b_c[CHUNK - 1:CHUNK]
        qs_c = qf[rs] * jnp.exp(b_c - b_mid)
        ks_c = kf[rs] * jnp.exp(b_mid - b_c)
        q0_c = qs_c * jnp.exp(b_mid)
        kh_c = ks_c * jnp.exp(b_last - b_mid)
        dec_c = jnp.exp(b_last)
        v_c = v[rs]
        for h in range(HG_HEADS):
            sl = slice(h * HG_DK, (h + 1) * HG_DK)
            for lst, val in ((qs, qs_c), (ks, ks_c), (q0, q0_c), (kh, kh_c), (vv, v_c),
                             (dec, dec_c)):
                lst.append(val[:, sl])
    items = range(nchunk * HG_HEADS)
    yield
    scores = [jnp.where(causal, paced(_dot_nt(qs[i], ks[i])), 0.0) for i in items]
    kv = [paced(_dot_tn(vv[i], kh[i])) for i in items]
    o_in = [paced(_dot(scores[i], vv[i])) for i in items]
    yield
    st = [st_ref[h] for h in range(HG_HEADS)]
    o_rows = []
    for c in range(nchunk):
        o_c = []
        for h in range(HG_HEADS):
            i = c * HG_HEADS + h
            o = o_in[i] + paced(_dot_nt(q0[i], st[h]))
            st[h] = st[h] * dec[i] + kv[i]
            o_c.append(o * lax.rsqrt(jnp.mean(o * o, axis=-1, keepdims=True) + EPS))
        o_rows.append(jnp.concatenate(o_c, axis=1))
    for h in range(HG_HEADS):
        st_ref[h] = st[h]
    return jnp.concatenate(o_rows, axis=0) * gate


def _rwkv7_block(p_ref, mix_ref, w0_ref, a0_ref, lora_ref, g2_ref, kk_ref, ka_ref, rk_ref,
                 lnw_ref, lnb_ref, tril, seg, s_ref, prev_ref, paced):
    rows = p_ref.shape[0]
    nchunk = rows // CHUNK
    p = p_ref[...]
    rowi = lax.broadcasted_iota(jnp.int32, (rows, RW_COLS), 0)
    shifted = jnp.where(rowi == 0, prev_ref[...], pltpu.roll(p, 1, axis=0))
    prev_ref[...] = p[rows - 1:rows]
    pf = p + (shifted - p) * mix_ref[...]

    r = pf[:, 0:RW_WIDTH]
    k = pf[:, RW_WIDTH:2 * RW_WIDTH]
    v = pf[:, 2 * RW_WIDTH:3 * RW_WIDTH]
    d_wa = pf[:, 3 * RW_WIDTH:3 * RW_WIDTH + LANES]
    d_g = pf[:, 3 * RW_WIDTH + LANES:3 * RW_WIDTH + 2 * LANES]
    lane_r = lax.broadcasted_iota(jnp.int32, (rows, LANES), 1)
    lora = _dot(jnp.where(lane_r < RW_HEAD, jnp.tanh(d_wa), d_wa), lora_ref[...])
    u = w0_ref[...] + lora[:, 0:RW_WIDTH]
    ld = -math.exp(-0.5) * _sigmoid(u)
    alpha = _sigmoid(a0_ref[...] + lora[:, RW_WIDTH:2 * RW_WIDTH])
    gate = _dot(_sigmoid(d_g), g2_ref[...])

    kk = k * kk_ref[...]
    n2 = _head_sums(kk * kk, seg)
    kk = kk * jnp.minimum(lax.rsqrt(n2), 1e12)
    km = k * (1.0 + (alpha - 1.0) * ka_ref[...])
    cum = _cumsum_rows(tril, ld)
    cum_prev = cum - ld
    kka = kk * alpha
    yield

    lane = lax.broadcasted_iota(jnp.int32, (CHUNK, LANES), 1)
    m0 = lane < RW_HEAD
    t_i = lax.broadcasted_iota(jnp.int32, (CHUNK, LANES), 0)
    s_i = lane & (RW_HEAD - 1)
    strict = s_i < t_i
    incl = s_i <= t_i
    eye = (s_i == t_i).astype(F32)
    levels = [((t_i >> (j + 1)) == (s_i >> (j + 1))) & ((t_i >> j) != (s_i >> j)) & strict
              for j in range(6)]
    r128 = lax.broadcasted_iota(jnp.int32, (LANES, LANES), 0)
    c128 = lax.broadcasted_iota(jnp.int32, (LANES, LANES), 1)
    same_head = (r128 >= RW_HEAD) == (c128 >= RW_HEAD)
    diag128 = r128 == c128
    zeros = jnp.zeros((CHUNK, LANES), F32)

    def bd(x):
        return jnp.concatenate([jnp.where(m0, x, 0.0), jnp.where(m0, 0.0, x)], axis=0)

    ac, bc, kc, rc, a0, r0, bh, kh, vv, gam = ([] for _ in range(10))
    for c in range(nchunk):
        rs = slice(c * CHUNK, (c + 1) * CHUNK)
        cum_c = cum[rs]
        c_mid = cum_c[CHUNK // 2 - 1:CHUNK // 2]
        c_last = cum_c[CHUNK - 1:CHUNK]
        e_mid = jnp.exp(c_mid)
        e_tail = jnp.exp(c_last - c_mid)
        gamma = jnp.exp(c_last)
        dn = jnp.exp(c_mid - cum_c)
        ac_c = -kk[rs] * jnp.exp(cum_prev[rs] - c_mid)
        bc_c = kka[rs] * dn
        kc_c = km[rs] * dn
        rc_c = r[rs] * jnp.exp(cum_c - c_mid)
        a0_c = ac_c * e_mid
        r0_c = rc_c * e_mid
        bh_c = bc_c * e_tail
        kh_c = kc_c * e_tail
        v_c = v[rs]
        for pr in range(PAIRS):
            sl = slice(pr * LANES, (pr + 1) * LANES)
            for lst, val in ((ac, ac_c), (bc, bc_c), (kc, kc_c), (rc, rc_c), (a0, a0_c), (r0, r0_c),
                             (bh, bh_c), (kh, kh_c), (vv, v_c), (gam, gamma)):
                lst.append(val[:, sl])
    chains = range(nchunk * PAIRS)
    yield

    gram = [paced(_dot_nt(jnp.concatenate([ac[i], rc[i]], axis=0),
                          jnp.concatenate([bd(bc[i]), bd(kc[i])], axis=0)))
            for i in chains]
    n_ab = [jnp.where(strict, gram[i][0:CHUNK, 0:LANES], 0.0) for i in chains]
    m_ak = [jnp.where(strict, gram[i][0:CHUNK, LANES:2 * LANES], 0.0) for i in chains]
    p_rb = [jnp.where(incl, gram[i][CHUNK:2 * CHUNK, 0:LANES], 0.0) for i in chains]
    p_rk = [jnp.where(incl, gram[i][CHUNK:2 * CHUNK, LANES:2 * LANES], 0.0) for i in chains]
    xv = [paced(_dot(jnp.concatenate([m_ak[i], p_rk[i]], axis=0), bd(vv[i]))) for i in chains]
    x1 = [xv[i][0:CHUNK] for i in chains]
    yield
    tinv = [eye + jnp.where(levels[0], n_ab[i], 0.0) for i in chains]
    for j in range(1, 6):
        fj = [paced(_dot(jnp.where(levels[j], n_ab[i], 0.0), bd(tinv[i]))) for i in chains]
        tinv = [tinv[i] + paced(_dot(tinv[i], bd(fj[i]))) for i in chains]
        if j in (2, 4):
            yield
    uw = [paced(_dot(tinv[i], jnp.concatenate([bd(x1[i]), bd(a0[i])], axis=1)))
          for i in chains]
    yield
    u_loc = [uw[i][:, 0:LANES] for i in chains]
    w_eff = [uw[i][:, LANES:2 * LANES] for i in chains]
    yq = [paced(_dot(p_rb[i], jnp.concatenate([bd(u_loc[i]), bd(w_eff[i])], axis=1)))
          for i in chains]
    y_loc = [yq[i][:, 0:LANES] + xv[i][CHUNK:2 * CHUNK] for i in chains]
    gd = [paced(_dot_tn(jnp.concatenate([jnp.concatenate([u_loc[i], w_eff[i]], axis=1),
                                         jnp.concatenate([vv[i], zeros], axis=1)], axis=0),
                        jnp.concatenate([bh[i], kh[i]], axis=0))) for i in chains]
    d_t = [jnp.where(m0, gd[i][0:CHUNK], gd[i][CHUNK:LANES]) for i in chains]
    g_t = [jnp.where(same_head, gd[i][LANES:2 * LANES], 0.0)
           + jnp.where(diag128, jnp.broadcast_to(gam[i], (LANES, LANES)), 0.0) for i in chains]
    q_eff = [r0[i] + yq[i][:, LANES:2 * LANES] for i in chains]

    st = [s_ref[pr] for pr in range(PAIRS)]
    y_rows = []
    for c in range(nchunk):
        st_in = st
        st = [paced(_dot(st_in[pr], g_t[c * PAIRS + pr])) + d_t[c * PAIRS + pr]
              for pr in range(PAIRS)]
        y_rows.append(jnp.concatenate(
            [paced(_dot_nt(q_eff[c * PAIRS + pr], bd(st_in[pr]))) + y_loc[c * PAIRS + pr]
             for pr in range(PAIRS)], axis=1))
    for pr in range(PAIRS):
        s_ref[pr] = st[pr]
    y = jnp.concatenate(y_rows, axis=0)

    inv_n = 1.0 / RW_HEAD
    mu = _head_sums(y, seg) * inv_n
    paced(None, burst=TAIL_BURST)
    yc = y - mu
    var = _head_sums(yc * yc, seg) * inv_n
    paced(None, burst=TAIL_BURST)
    yn = yc * lax.rsqrt(var + RW_GN_EPS) * lnw_ref[...] + lnb_ref[...]
    bonus = _head_sums(r * km * rk_ref[...], seg) * v
    return (yn + bonus) * gate


def _mixer_kernel(nblk, x_ref, n1w_ref, win_ref, lbp_ref, hgnw_ref, mix_ref, w0_ref, a0_ref, lora_ref,
                  g2_ref, kk_ref, ka_ref, rk_ref, lnw_ref, lnb_ref, tril_ref, seg_ref,
                  wo_ref, w1_ref, w3_ref, w2_ref, o_ref, wo_bf, w1_bf, w3_bf, w2_bf,
                  proj_a, proj_b, hst_ref, s_ref, prev_ref):
    t = pl.program_id(0)
    for src, dst in ((wo_ref, wo_bf), (w1_ref, w1_bf), (w3_ref, w3_bf), (w2_ref, w2_bf)):
        dst[...] = src[...].astype(BF16)

    @pl.when(t == 0)
    def _():
        proj_a[...] = jnp.dot(_rmsnorm_bf16(x_ref[...], n1w_ref[...]), win_ref[...],
                              preferred_element_type=F32)

    @pl.when((t > 0) & ((t - 1) % nblk == 0))
    def _():
        hst_ref[...] = jnp.zeros_like(hst_ref)
        s_ref[...] = jnp.zeros_like(s_ref)
        prev_ref[...] = jnp.zeros_like(prev_ref)

    def finish(gen):
        try:
            next(gen)
        except StopIteration as done:
            return done.value
        raise AssertionError("mixer generator has more phases than the schedule below")

    def step(p_out, p_in):
        h = _rmsnorm_bf16(x_ref[...], n1w_ref[...])
        tiles = iter([(n, k) for n in range(0, IN_COLS, MXU_WIDTH)
                      for k in range(0, D_MODEL, MXU_WIDTH)])
        partial = [None]

        def project(count):
            for n, k in [t for _, t in zip(range(count), tiles)]:
                part = jnp.dot(h[:, k:k + MXU_WIDTH], win_ref[k:k + MXU_WIDTH, n:n + MXU_WIDTH],
                               preferred_element_type=F32)
                partial[0] = part if k == 0 else partial[0] + part
                if k + MXU_WIDTH == D_MODEL:
                    p_out[:, n:n + MXU_WIDTH] = partial[0]

        calls = [0]
        paced_tiles = [IN_COLS * D_MODEL // MXU_WIDTH ** 2 - PROLOGUE_TILES - 2 * TAIL_BURST]

        def paced(value, burst=0):
            if burst:
                project(burst)
                return value
            calls[0] += 1
            if calls[0] % PACE == 0 and paced_tiles[0] > 0:
                paced_tiles[0] -= 1
                project(1)
            return value

        tril = tril_ref[...]
        hg = _hgrn2_block(p_in.at[:, 0:HG_COLS], lbp_ref, hgnw_ref, tril, hst_ref, paced)
        rw = _rwkv7_block(p_in.at[:, HG_COLS:IN_COLS], mix_ref, w0_ref, a0_ref, lora_ref, g2_ref,
                          kk_ref, ka_ref, rk_ref, lnw_ref, lnb_ref, tril, seg_ref[...], s_ref,
                          prev_ref, paced)
        project(PROLOGUE_TILES // 2)
        next(rw)
        project(PROLOGUE_TILES - PROLOGUE_TILES // 2)
        next(rw)
        next(rw)
        next(hg)
        next(rw)
        next(hg)
        next(rw)
        o_ref[:, 0:HG_WIDTH] = finish(hg).astype(o_ref.dtype)
        next(rw)
        o_ref[:, HG_WIDTH:D_MODEL] = finish(rw).astype(o_ref.dtype)
        project(IN_COLS * D_MODEL // MXU_WIDTH ** 2)

    @pl.when((t % 2 == 0) & (t > 0))
    def _():
        step(proj_a, proj_b)

    @pl.when(t % 2 == 1)
    def _():
        step(proj_b, proj_a)


def _cast_block_rows(rows, steps):
    for block in range(BF16_SUBLANES, rows + 1, BF16_SUBLANES):
        if rows % block == 0 and rows // block <= steps:
            return block
    raise ValueError(f"cannot convert {rows} weight rows in {steps} grid steps")


def _mixers(x, norm1_w, w_in, lbp, hg_norm_w, mix, w0, a0, lora_w, g2, k_k, k_a, r_k, ln_w, ln_b,
            tril, seg, ffn_weights):
    batch, seq, _ = x.shape
    rows = MIX_BLOCK_CHUNKS * CHUNK
    nblk = seq // rows
    total = batch * nblk
    steps = total + 1
    const = lambda shape: pl.BlockSpec(shape, lambda t: (0, 0), pipeline_mode=pl.Buffered(1))
    vec = lambda n: const((1, n))

    def block_of(i):
        return i // nblk, i % nblk, 0

    def cast_spec(w):
        block = _cast_block_rows(w.shape[0], steps)
        last = w.shape[0] // block - 1
        return pl.BlockSpec((block, w.shape[1]), lambda t: (jnp.minimum(t, last), 0))

    cast_specs = [cast_spec(w) for w in ffn_weights]
    return pl.pallas_call(
        functools.partial(_mixer_kernel, nblk),
        grid=(steps,),
        in_specs=[
            pl.BlockSpec((None, rows, D_MODEL), lambda t: block_of(jnp.minimum(t, total - 1))),
            vec(D_MODEL), const((D_MODEL, IN_COLS)),
            const((2, HG_WIDTH)), vec(HG_WIDTH),
            vec(RW_COLS), vec(RW_WIDTH), vec(RW_WIDTH),
            const((LANES, 2 * RW_WIDTH)), const((LANES, RW_WIDTH)),
            vec(RW_WIDTH), vec(RW_WIDTH), vec(RW_WIDTH), vec(RW_WIDTH), vec(RW_WIDTH),
            const(tril.shape), const((MXU_WIDTH, MXU_WIDTH)),
        ] + cast_specs,
        out_specs=[pl.BlockSpec((None, rows, D_MODEL),
                                lambda t: block_of(jnp.maximum(t - 1, 0)))] + cast_specs,
        out_shape=[jax.ShapeDtypeStruct((batch, seq, D_MODEL), BF16)]
        + [jax.ShapeDtypeStruct(w.shape, BF16) for w in ffn_weights],
        scratch_shapes=[pltpu.VMEM((rows, IN_COLS), F32), pltpu.VMEM((rows, IN_COLS), F32),
                        pltpu.VMEM((HG_HEADS, HG_DK, HG_DK), F32),
                        pltpu.VMEM((PAIRS, RW_HEAD, LANES), F32),
                        pltpu.VMEM((1, RW_COLS), F32)],
        compiler_params=pltpu.CompilerParams(
            dimension_semantics=("arbitrary",), vmem_limit_bytes=VMEM_LIMIT),
        name="mixers",
    )(x, norm1_w, w_in, lbp, hg_norm_w, mix, w0, a0, lora_w, g2, k_k, k_a, r_k, ln_w, ln_b,
      tril, seg, *ffn_weights)


def _ffn_kernel(ntiles, x_ref, o_ref, wo_ref, n2_ref, w1_ref, w3_ref, w2_ref, fn_ref, out_ref,
                x2_scr):
    i = pl.program_id(0)

    def finalize():
        x2 = x2_scr[...]
        ms2 = jnp.mean(x2 * x2, axis=-1, keepdims=True)
        out_ref[...] = x2 * lax.rsqrt(ms2 + EPS) * fn_ref[...]

    @pl.when(i == 0)
    def _():
        x2_scr[...] = jnp.zeros_like(x2_scr)

    @pl.when(i < ntiles)
    def _():
        x1 = x_ref[...] + jnp.dot(o_ref[...], wo_ref[...], preferred_element_type=F32)
        h = _rmsnorm_bf16(x1, n2_ref[...])
        a = jnp.dot(h, w1_ref[...], preferred_element_type=F32)
        finalize()
        b = jnp.dot(h, w3_ref[...], preferred_element_type=F32)
        hm = (_silu(a) * b).astype(BF16)
        x2_scr[...] = x1 + jnp.dot(hm, w2_ref[...], preferred_element_type=F32)

    @pl.when(i == ntiles)
    def _():
        finalize()


def _ffn(x2, o_mix, w_out, norm2_w, w1, w3, w2, final_w, tm):
    rows = x2.shape[0]
    ntiles = rows // tm
    const = lambda r, c_: pl.BlockSpec((r, c_), lambda i: (0, 0), pipeline_mode=pl.Buffered(1))
    tile_in = pl.BlockSpec((tm, D_MODEL), lambda i: (jnp.minimum(i, ntiles - 1), 0))
    return pl.pallas_call(
        functools.partial(_ffn_kernel, ntiles),
        grid=(ntiles + 1,),
        in_specs=[
            tile_in, tile_in,
            const(D_MODEL, D_MODEL), const(1, D_MODEL),
            const(D_MODEL, D_FF), const(D_MODEL, D_FF), const(D_FF, D_MODEL),
            const(1, D_MODEL),
        ],
        out_specs=pl.BlockSpec((tm, D_MODEL), lambda i: (jnp.maximum(i - 1, 0), 0)),
        out_shape=jax.ShapeDtypeStruct((rows, D_MODEL), F32),
        scratch_shapes=[pltpu.VMEM((tm, D_MODEL), F32)],
        compiler_params=pltpu.CompilerParams(
            dimension_semantics=("arbitrary",), vmem_limit_bytes=VMEM_LIMIT),
        name="outproj_ffn",
    )(x2, o_mix, w_out, norm2_w, w1, w3, w2, final_w)


def kernel(x, norm1_w, w_in, hg_lower_bounds, hg_norm_w, rw_mix, rw_w0, rw_w2, rw_a0, rw_a2,
           rw_g2, rw_k_k, rw_k_a, rw_r_k, rw_ln_w, rw_ln_b, w_out, norm2_w, ffn_w1, ffn_w3,
           ffn_w2, final_norm_w):
    batch, seq, _ = x.shape
    rows = batch * seq
    row = lambda z: z.reshape(1, -1)

    tril = jnp.kron(jnp.eye(MXU_WIDTH // 2 // CHUNK, dtype=BF16),
                    jnp.tril(jnp.ones((CHUNK, CHUNK), BF16)))
    tril = jnp.concatenate([tril, tril], axis=1)
    zero = jnp.zeros((RW_LORA, RW_WIDTH), F32)
    lora_w = jnp.concatenate([jnp.concatenate([rw_w2[0], zero], axis=1),
                              jnp.concatenate([zero, rw_a2[0]], axis=1)], axis=0).astype(BF16)
    head_id = jnp.arange(MXU_WIDTH) // RW_HEAD
    seg = (head_id[:, None] == head_id[None, :]).astype(BF16)
    o_mix, wo_bf, w1_bf, w3_bf, w2_bf = _mixers(
        x, row(norm1_w[0]), w_in[0].astype(BF16), hg_lower_bounds, row(hg_norm_w[0]),
        row(rw_mix[0]), row(rw_w0[0]), row(rw_a0[0]), lora_w, rw_g2[0].astype(BF16),
        row(rw_k_k[0]), row(rw_k_a[0]), row(rw_r_k[0]), row(rw_ln_w[0]), row(rw_ln_b[0]), tril, seg,
        (w_out[0], ffn_w1[0], ffn_w3[0], ffn_w2[0]))

    out = _ffn(x.reshape(rows, D_MODEL), o_mix.reshape(rows, D_MODEL), wo_bf, row(norm2_w[0]),
               w1_bf, w3_bf, w2_bf, row(final_norm_w), tm=512)
    return out.reshape(batch, seq, D_MODEL)
```
